```python
import math
import jax
import jax.numpy as jnp
from jax import lax
import numpy as np

D_MODEL = 4096
BATCH = 4
SEQ = 2048
DEPTH = 4
DEC_BATCH = 128
DEC_SEQ = 1
PAST_LEN = 16384
PAGE_SIZE = 128

N_MIXERS = 4
CHUNK = 128
NORM_EPS = 1e-6
F32 = jnp.float32

RET_HEADS = 16
RET_DK = D_MODEL // RET_HEADS
RET_DV = 2 * D_MODEL // RET_HEADS
RET_QK_DIM = RET_HEADS * RET_DK
RET_V_DIM = RET_HEADS * RET_DV
RET_IN_DIM = 2 * RET_QK_DIM + 2 * RET_V_DIM
ROPE_BASE = 10000.0

SSM_D_INNER = 2 * D_MODEL
SSM_HEAD_DIM = 64
SSM_HEADS = SSM_D_INNER // SSM_HEAD_DIM
SSM_STATE = 128
SSM_GROUPS = 8
SSM_CONV = 4
SSM_CONV_DIM = SSM_D_INNER + 2 * SSM_GROUPS * SSM_STATE
SSM_IN_DIM = SSM_D_INNER + SSM_CONV_DIM + SSM_HEADS

GDN_K_HEADS = D_MODEL // 128
GDN_V_HEADS = 2 * GDN_K_HEADS
GDN_DK = 128
GDN_DV = 128
GDN_CONV = 4
GDN_KEY_DIM = GDN_K_HEADS * GDN_DK
GDN_VAL_DIM = GDN_V_HEADS * GDN_DV
GDN_CONV_DIM = 2 * GDN_KEY_DIM + GDN_VAL_DIM
GDN_IN_DIM = GDN_CONV_DIM + GDN_VAL_DIM + 2 * GDN_V_HEADS

ML_HEADS = 8
ML_DQK = D_MODEL // (2 * ML_HEADS)
ML_DV = D_MODEL // ML_HEADS
ML_QK_DIM = ML_HEADS * ML_DQK
ML_V_DIM = ML_HEADS * ML_DV
ML_IN_DIM = 2 * ML_QK_DIM + 2 * ML_V_DIM + 2 * ML_HEADS

D_FF = ((8 * D_MODEL // 3 + 255) // 256) * 256
FFN_CONV = 3

kernel_name = 'hybrid_ret_ssd_gdn_mlstm_convffn_step'


def _rmsnorm(x, w=None):
    xf = x.astype(F32)
    y = xf * lax.rsqrt(jnp.mean(xf * xf, axis=-1, keepdims=True) + NORM_EPS)
    if w is not None:
        y = y * w.astype(F32)
    return y.astype(x.dtype)


def _l2norm(x):
    return x * lax.rsqrt(jnp.sum(x * x, axis=-1, keepdims=True) + NORM_EPS)


def _rope(x, pos):
    half = x.shape[-1] // 2
    inv_freq = 1.0 / (ROPE_BASE ** jnp.linspace(0.0, 1.0, half, dtype=F32))
    ang = pos.astype(F32)[:, None] * inv_freq[None, :]
    cos = jnp.cos(ang)[None, :, None, :]
    sin = jnp.sin(ang)[None, :, None, :]
    x = x.astype(F32)
    x1, x2 = x[..., :half], x[..., half:]
    return jnp.concatenate([x1 * cos - x2 * sin, x1 * sin + x2 * cos], axis=-1)


def _causal_dwconv(x, buf, w, b=None):
    width, t = w.shape[0], x.shape[1]
    xp = jnp.concatenate([buf.astype(x.dtype), x], axis=1)
    y = xp[:, 0:t] * w[0]
    for j in range(1, width):
        y = y + xp[:, j:j + t] * w[j]
    if b is not None:
        y = y + b
    return y, xp[:, xp.shape[1] - (width - 1):]


def _chunk_len(t):
    return CHUNK if t % CHUNK == 0 else t


def _to_chunks(a, L):
    b, t = a.shape[:2]
    return jnp.moveaxis(a.reshape(b, t // L, L, *a.shape[2:]), 1, 0)


def _from_chunks(a):
    nc, b, L = a.shape[:3]
    return jnp.moveaxis(a, 0, 1).reshape(b, nc * L, *a.shape[3:])


def _retention(h, s0, pos0, w_in, w_out):
    b, t, _ = h.shape
    q, k, v, g = jnp.split(h @ w_in, [RET_QK_DIM, 2 * RET_QK_DIM, 2 * RET_QK_DIM + RET_V_DIM], axis=-1)
    pos = pos0 + jnp.arange(t)
    q = _rope(q.reshape(b, t, RET_HEADS, RET_DK), pos)
    k = _rope(k.reshape(b, t, RET_HEADS, RET_DK), pos) * (RET_DK ** -0.5)
    v = v.reshape(b, t, RET_HEADS, RET_DV).astype(F32)
    log_gamma = jnp.log(1.0 - jnp.exp2(-5.0 - jnp.arange(RET_HEADS, dtype=F32)))
    L = _chunk_len(t)
    idx = jnp.arange(L, dtype=F32)
    rel = idx[:, None] - idx[None, :]
    decay_in = jnp.where(rel >= 0, jnp.exp(log_gamma[:, None, None] * jnp.maximum(rel, 0.0)), 0.0)
    decay_q = jnp.exp((idx[:, None] + 1.0) * log_gamma[None, :])
    decay_k = jnp.exp((L - 1.0 - idx[:, None]) * log_gamma[None, :])
    decay_s = jnp.exp(L * log_gamma)

    def step(s, inp):
        qc, kc, vc = inp
        sc = jnp.einsum('bihd,bjhd->bhij', qc, kc) * decay_in
        o = (jnp.einsum('bhij,bjhv->bihv', sc, vc)
             + jnp.einsum('bihd,bhdv->bihv', qc, s) * decay_q[None, :, :, None])
        s = decay_s[:, None, None] * s + jnp.einsum('bjhd,bjhv->bhdv', kc * decay_k[None, :, :, None], vc)
        return s, o

    s, o = lax.scan(step, s0.astype(F32), (_to_chunks(q, L), _to_chunks(k, L), _to_chunks(v, L)))
    o = _rmsnorm(_from_chunks(o)).reshape(b, t, RET_V_DIM)
    y = (jax.nn.silu(g.astype(F32)) * o).astype(h.dtype) @ w_out
    return y, s.astype(s0.dtype)


def _mamba2(h, s0, c0, w_in, conv_w, conv_b, dt_bias, a_log, d_skip, norm_w, w_out):
    b, t, _ = h.shape
    G, HPG, P, N = SSM_GROUPS, SSM_HEADS // SSM_GROUPS, SSM_HEAD_DIM, SSM_STATE
    z, xbc, dt_raw = jnp.split(h @ w_in, [SSM_D_INNER, SSM_D_INNER + SSM_CONV_DIM], axis=-1)
    xbc, c_new = _causal_dwconv(xbc, c0, conv_w, conv_b)
    xbc = jax.nn.silu(xbc.astype(F32))
    xs, bm, cm = jnp.split(xbc, [SSM_D_INNER, SSM_D_INNER + G * N], axis=-1)
    xs = xs.reshape(b, t, G, HPG, P)
    bm = bm.reshape(b, t, G, N)
    cm = cm.reshape(b, t, G, N)
    dt = jax.nn.softplus(dt_raw.astype(F32) + dt_bias).reshape(b, t, G, HPG)
    la = dt * (-jnp.exp(a_log.astype(F32))).reshape(G, HPG)
    L = _chunk_len(t)
    incl = jnp.tril(jnp.ones((L, L), dtype=bool))

    def step(st, inp):
        xc, bc, cc, dtc, ac = inp
        acs = jnp.cumsum(ac, axis=1)
        at = jnp.moveaxis(acs, 1, -1)
        decay = jnp.exp(jnp.where(incl, at[..., :, None] - at[..., None, :], -jnp.inf))
        cb = jnp.einsum('bign,bjgn->bgij', cc, bc)
        scores = cb[:, :, None] * decay * jnp.moveaxis(dtc, 1, -1)[..., None, :]
        y = (jnp.einsum('bghij,bjghp->bighp', scores, xc)
             + jnp.einsum('bign,bghpn->bighp', cc, st) * jnp.exp(acs)[..., None])
        w_end = jnp.exp(acs[:, -1:] - acs) * dtc
        st = jnp.exp(acs[:, -1])[..., None, None] * st + jnp.einsum('bjghp,bjgh,bjgn->bghpn', xc, w_end, bc)
        return st, y

    s_init = s0.astype(F32).reshape(s0.shape[0], G, HPG, P, N)
    s, y = lax.scan(step, s_init, tuple(_to_chunks(a, L) for a in (xs, bm, cm, dt, la)))
    y = _from_chunks(y) + d_skip.astype(F32).reshape(G, HPG, 1) * xs
    y = y.reshape(b, t, SSM_D_INNER) * jax.nn.silu(z.astype(F32))
    y = _rmsnorm(y.reshape(b, t, G, -1), norm_w.reshape(G, -1)).reshape(b, t, SSM_D_INNER)
    return y.astype(h.dtype) @ w_out, s.reshape(s0.shape).astype(s0.dtype), c_new


def _gated_deltanet(h, s0, c0, w_in, conv_w, a_log, dt_bias, norm_w, w_out):
    b, t, _ = h.shape
    qkv, z, beta_raw, a_raw = jnp.split(
        h @ w_in, [GDN_CONV_DIM, GDN_CONV_DIM + GDN_VAL_DIM, GDN_CONV_DIM + GDN_VAL_DIM + GDN_V_HEADS], axis=-1)
    qkv, c_new = _causal_dwconv(qkv, c0, conv_w)
    qkv = jax.nn.silu(qkv.astype(F32))
    q, k, v = jnp.split(qkv, [GDN_KEY_DIM, 2 * GDN_KEY_DIM], axis=-1)
    rep = GDN_V_HEADS // GDN_K_HEADS
    q = jnp.repeat(_l2norm(q.reshape(b, t, GDN_K_HEADS, GDN_DK)), rep, axis=2) * (GDN_DK ** -0.5)
    k = jnp.repeat(_l2norm(k.reshape(b, t, GDN_K_HEADS, GDN_DK)), rep, axis=2)
    v = v.reshape(b, t, GDN_V_HEADS, GDN_DV)
    beta = jax.nn.sigmoid(beta_raw.astype(F32))
    g = -jnp.exp(a_log.astype(F32)) * jax.nn.softplus(a_raw.astype(F32) + dt_bias)
    L = _chunk_len(t)
    idx = jnp.arange(L)
    strict = idx[:, None] > idx[None, :]
    incl = idx[:, None] >= idx[None, :]

    def step(s, inp):
        qc, kc, vc, bc, gc = inp
        gt = jnp.moveaxis(jnp.cumsum(gc, axis=1), 1, -1)
        bt = jnp.moveaxis(bc, 1, -1)
        seg = gt[..., :, None] - gt[..., None, :]
        dec_strict = jnp.exp(jnp.where(strict, seg, -jnp.inf))
        dec_incl = jnp.exp(jnp.where(incl, seg, -jnp.inf))
        a_mat = jnp.einsum('bihd,bjhd->bhij', kc, kc) * dec_strict * bt[..., :, None]
        vt = jnp.moveaxis(vc, 1, 2)
        ks = jnp.einsum('bihd,bhdv->bhiv', kc, s)
        rhs = bt[..., None] * (vt - jnp.exp(gt)[..., None] * ks)
        delta = lax.linalg.triangular_solve(a_mat, rhs, left_side=True, lower=True, unit_diagonal=True)
        qk = jnp.einsum('bihd,bjhd->bhij', qc, kc) * dec_incl
        o = (jnp.exp(gt)[..., None] * jnp.einsum('bihd,bhdv->bhiv', qc, s)
             + jnp.einsum('bhij,bhjv->bhiv', qk, delta))
        w_end = jnp.exp(gt[..., -1:] - gt)
        s = jnp.exp(gt[..., -1])[..., None, None] * s + jnp.einsum('bjhd,bhj,bhjv->bhdv', kc, w_end, delta)
        return s, jnp.moveaxis(o, 1, 2)

    s, o = lax.scan(step, s0.astype(F32), tuple(_to_chunks(a, L) for a in (q, k, v, beta, g)))
    o = _rmsnorm(_from_chunks(o), norm_w) * jax.nn.silu(z.astype(F32)).reshape(b, t, GDN_V_HEADS, GDN_DV)
    return o.reshape(b, t, GDN_VAL_DIM).astype(h.dtype) @ w_out, s.astype(s0.dtype), c_new


def _mlstm(h, c0, n0, m0, w_in, gate_b, norm_w, w_out):
    b, t, _ = h.shape
    H = ML_HEADS
    q, k, v, o_raw, i_raw, f_raw = jnp.split(
        h @ w_in, [ML_QK_DIM, 2 * ML_QK_DIM, 2 * ML_QK_DIM + ML_V_DIM, 2 * ML_QK_DIM + 2 * ML_V_DIM,
                   2 * ML_QK_DIM + 2 * ML_V_DIM + H], axis=-1)
    q = q.reshape(b, t, H, ML_DQK).astype(F32)
    k = k.reshape(b, t, H, ML_DQK).astype(F32) * (ML_DQK ** -0.5)
    v = v.reshape(b, t, H, ML_DV).astype(F32)
    log_i = i_raw.astype(F32) + gate_b[:H]
    log_f = jax.nn.log_sigmoid(f_raw.astype(F32) + gate_b[H:])
    L = _chunk_len(t)
    idx = jnp.arange(L)
    incl = idx[:, None] >= idx[None, :]

    def step(carry, inp):
        c, n, m = carry
        qc, kc, vc, ic, fc = inp
        ft = jnp.moveaxis(jnp.cumsum(fc, axis=1), 1, -1)
        it = jnp.moveaxis(ic, 1, -1)
        dmat = jnp.where(incl, ft[..., :, None] - ft[..., None, :] + it[..., None, :], -jnp.inf)
        inter = ft + m[..., None]
        mi = jnp.maximum(inter, jnp.max(dmat, axis=-1))
        wmat = jnp.exp(dmat - mi[..., None])
        winter = jnp.exp(inter - mi)
        sc = jnp.einsum('bihd,bjhd->bhij', qc, kc) * wmat
        num = winter[..., None] * jnp.einsum('bihd,bhdv->bhiv', qc, c) + jnp.einsum('bhij,bjhv->bhiv', sc, vc)
        den = winter * jnp.einsum('bihd,bhd->bhi', qc, n) + jnp.sum(sc, axis=-1)
        hh = num / jnp.maximum(jnp.abs(den), jnp.exp(-mi))[..., None]
        m_new = mi[..., -1]
        wk = jnp.exp(ft[..., -1:] - ft + it - m_new[..., None])
        keep = jnp.exp(ft[..., -1] + m - m_new)
        c = keep[..., None, None] * c + jnp.einsum('bjhd,bhj,bjhv->bhdv', kc, wk, vc)
        n = keep[..., None] * n + jnp.einsum('bjhd,bhj->bhd', kc, wk)
        return (c, n, m_new), jnp.moveaxis(hh, 1, 2)

    (c, n, m), hh = lax.scan(step, (c0.astype(F32), n0.astype(F32), m0.astype(F32)),
                             tuple(_to_chunks(a, L) for a in (q, k, v, log_i, log_f)))
    hh = _rmsnorm(_from_chunks(hh), norm_w) * jax.nn.sigmoid(o_raw.astype(F32)).reshape(b, t, H, ML_DV)
    y = hh.reshape(b, t, ML_V_DIM).astype(h.dtype) @ w_out
    return y, c.astype(c0.dtype), n.astype(n0.dtype), m.astype(m0.dtype)


def _conv_ffn(h, buf, w_gate, w_up, conv_w, conv_b, w_down):
    gate, buf_new = _causal_dwconv(h @ w_gate, buf, conv_w, conv_b)
    return (jax.nn.silu(gate) * (h @ w_up)) @ w_down, buf_new


def _trunk(x, pos0, s_ret, s_ssm, s_ssm_conv, s_gdn, s_gdn_conv, s_c, s_n, s_m, s_ffn, p):
    ffn_bufs = []
    for layer in range(DEPTH):
        kind = layer % N_MIXERS
        hn = _rmsnorm(x, p['norm_mix'][layer])
        if kind == 0:
            y, s_ret = _retention(hn, s_ret, pos0, p['w_ret_in'], p['w_ret_out'])
        elif kind == 1:
            y, s_ssm, s_ssm_conv = _mamba2(hn, s_ssm, s_ssm_conv, p['w_ssm_in'], p['ssm_conv_w'], p['ssm_conv_b'],
                                           p['ssm_dt_bias'], p['ssm_a_log'], p['ssm_d_skip'], p['ssm_norm_w'],
                                           p['w_ssm_out'])
        elif kind == 2:
            y, s_gdn, s_gdn_conv = _gated_deltanet(hn, s_gdn, s_gdn_conv, p['w_gdn_in'], p['gdn_conv_w'],
                                                   p['gdn_a_log'], p['gdn_dt_bias'], p['gdn_norm_w'], p['w_gdn_out'])
        else:
            y, s_c, s_n, s_m = _mlstm(hn, s_c, s_n, s_m, p['w_ml_in'], p['ml_gate_b'], p['ml_norm_w'], p['w_ml_out'])
        x = x + y
        hn = _rmsnorm(x, p['norm_ffn'][layer])
        y, buf = _conv_ffn(hn, s_ffn[layer], p['ffn_w_gate'][layer], p['ffn_w_up'][layer], p['ffn_conv_w'][layer],
                           p['ffn_conv_b'][layer], p['ffn_w_down'][layer])
        ffn_bufs.append(buf)
        x = x + y
    return (_rmsnorm(x, p['norm_final']), s_ret, s_ssm, s_ssm_conv, s_gdn, s_gdn_conv, s_c, s_n, s_m,
            jnp.stack(ffn_bufs))


def setup_inputs(seed: int = 0) -> dict:
    key = jax.random.key(seed)
    keys = iter(jax.random.split(key, 64))

    def nrm(shape, scale):
        return scale * jax.random.normal(next(keys), shape, F32)

    def gain(shape):
        return 1.0 + 0.02 * jax.random.normal(next(keys), shape, F32)

    def dt_bias(n):
        dt = jnp.exp(jax.random.uniform(next(keys), (n,), F32, math.log(1e-3), math.log(1e-1)))
        return dt + jnp.log(-jnp.expm1(-dt))

    def a_log(n):
        return jnp.log(jax.random.uniform(next(keys), (n,), F32, 1.0, 16.0))

    ml_gate_b = jnp.concatenate([-2.0 + nrm((ML_HEADS,), 0.1),
                                 jnp.linspace(3.0, 6.0, ML_HEADS, dtype=F32) + nrm((ML_HEADS,), 0.1)])
    d_in = D_MODEL ** -0.5
    return {
        'x_prompt': nrm((BATCH, SEQ, D_MODEL), 1.0),
        'x_sample': nrm((DEC_BATCH, DEC_SEQ, D_MODEL), 1.0),
        'state_ret': nrm((DEC_BATCH, RET_HEADS, RET_DK, RET_DV), RET_DK ** -0.5),
        'state_ssm': nrm((DEC_BATCH, SSM_HEADS, SSM_HEAD_DIM, SSM_STATE), 0.1),
        'state_ssm_conv': nrm((DEC_BATCH, SSM_CONV - 1, SSM_CONV_DIM), 1.0),
        'state_gdn': nrm((DEC_BATCH, GDN_V_HEADS, GDN_DK, GDN_DV), 0.1),
        'state_gdn_conv': nrm((DEC_BATCH, GDN_CONV - 1, GDN_CONV_DIM), 1.0),
        'state_mlstm_c': nrm((DEC_BATCH, ML_HEADS, ML_DQK, ML_DV), 0.1),
        'state_mlstm_n': nrm((DEC_BATCH, ML_HEADS, ML_DQK), 0.1),
        'state_mlstm_m': nrm((DEC_BATCH, ML_HEADS), 1.0),
        'state_ffn_conv': nrm((DEPTH, DEC_BATCH, FFN_CONV - 1, D_FF), 1.0),
        'norm_mix': gain((DEPTH, D_MODEL)),
        'norm_ffn': gain((DEPTH, D_MODEL)),
        'norm_final': gain((D_MODEL,)),
        'w_ret_in': nrm((D_MODEL, RET_IN_DIM), d_in),
        'w_ret_out': nrm((RET_V_DIM, D_MODEL), RET_V_DIM ** -0.5),
        'w_ssm_in': nrm((D_MODEL, SSM_IN_DIM), d_in),
        'ssm_conv_w': nrm((SSM_CONV, SSM_CONV_DIM), SSM_CONV ** -0.5),
        'ssm_conv_b': nrm((SSM_CONV_DIM,), 0.02),
        'ssm_dt_bias': dt_bias(SSM_HEADS),
        'ssm_a_log': a_log(SSM_HEADS),
        'ssm_d_skip': gain((SSM_HEADS,)),
        'ssm_norm_w': gain((SSM_D_INNER,)),
        'w_ssm_out': nrm((SSM_D_INNER, D_MODEL), SSM_D_INNER ** -0.5),
        'w_gdn_in': nrm((D_MODEL, GDN_IN_DIM), d_in),
        'gdn_conv_w': nrm((GDN_CONV, GDN_CONV_DIM), GDN_CONV ** -0.5),
        'gdn_a_log': a_log(GDN_V_HEADS),
        'gdn_dt_bias': dt_bias(GDN_V_HEADS),
        'gdn_norm_w': gain((GDN_DV,)),
        'w_gdn_out': nrm((GDN_VAL_DIM, D_MODEL), GDN_VAL_DIM ** -0.5),
        'w_ml_in': nrm((D_MODEL, ML_IN_DIM), d_in),
        'ml_gate_b': ml_gate_b,
        'ml_norm_w': gain((ML_HEADS, ML_DV)),
        'w_ml_out': nrm((ML_V_DIM, D_MODEL), ML_V_DIM ** -0.5),
        'ffn_w_gate': nrm((DEPTH, D_MODEL, D_FF), d_in),
        'ffn_w_up': nrm((DEPTH, D_MODEL, D_FF), d_in),
        'ffn_conv_w': nrm((DEPTH, FFN_CONV, D_FF), FFN_CONV ** -0.5),
        'ffn_conv_b': nrm((DEPTH, D_FF), 0.02),
        'ffn_w_down': nrm((DEPTH, D_FF, D_MODEL), D_FF ** -0.5),
    }


def reference(x_prompt, x_sample, state_ret, state_ssm, state_ssm_conv, state_gdn, state_gdn_conv,
              state_mlstm_c, state_mlstm_n, state_mlstm_m, state_ffn_conv, norm_mix, norm_ffn, norm_final,
              w_ret_in, w_ret_out, w_ssm_in, ssm_conv_w, ssm_conv_b, ssm_dt_bias, ssm_a_log, ssm_d_skip,
              ssm_norm_w, w_ssm_out, w_gdn_in, gdn_conv_w, gdn_a_log, gdn_dt_bias, gdn_norm_w, w_gdn_out,
              w_ml_in, ml_gate_b, ml_norm_w, w_ml_out, ffn_w_gate, ffn_w_up, ffn_conv_w, ffn_conv_b, ffn_w_down):
    params = {
        'norm_mix': norm_mix, 'norm_ffn': norm_ffn, 'norm_final': norm_final,
        'w_ret_in': w_ret_in, 'w_ret_out': w_ret_out,
        'w_ssm_in': w_ssm_in, 'ssm_conv_w': ssm_conv_w, 'ssm_conv_b': ssm_conv_b, 'ssm_dt_bias': ssm_dt_bias,
        'ssm_a_log': ssm_a_log, 'ssm_d_skip': ssm_d_skip, 'ssm_norm_w': ssm_norm_w, 'w_ssm_out': w_ssm_out,
        'w_gdn_in': w_gdn_in, 'gdn_conv_w': gdn_conv_w, 'gdn_a_log': gdn_a_log, 'gdn_dt_bias': gdn_dt_bias,
        'gdn_norm_w': gdn_norm_w, 'w_gdn_out': w_gdn_out,
        'w_ml_in': w_ml_in, 'ml_gate_b': ml_gate_b, 'ml_norm_w': ml_norm_w, 'w_ml_out': w_ml_out,
        'ffn_w_gate': ffn_w_gate, 'ffn_w_up': ffn_w_up, 'ffn_conv_w': ffn_conv_w, 'ffn_conv_b': ffn_conv_b,
        'ffn_w_down': ffn_w_down,
    }
    bp, dt = x_prompt.shape[0], x_prompt.dtype
    (y_prompt, prompt_ret, prompt_ssm, prompt_ssm_conv, prompt_gdn, prompt_gdn_conv, prompt_mlstm_c,
     prompt_mlstm_n, prompt_mlstm_m, prompt_ffn_conv) = _trunk(
        x_prompt, 0,
        jnp.zeros((bp, RET_HEADS, RET_DK, RET_DV), dt),
        jnp.zeros((bp, SSM_HEADS, SSM_HEAD_DIM, SSM_STATE), dt),
        jnp.zeros((bp, SSM_CONV - 1, SSM_CONV_DIM), dt),
        jnp.zeros((bp, GDN_V_HEADS, GDN_DK, GDN_DV), dt),
        jnp.zeros((bp, GDN_CONV - 1, GDN_CONV_DIM), dt),
        jnp.zeros((bp, ML_HEADS, ML_DQK, ML_DV), dt),
        jnp.zeros((bp, ML_HEADS, ML_DQK), dt),
        jnp.zeros((bp, ML_HEADS), dt),
        jnp.zeros((DEPTH, bp, FFN_CONV - 1, D_FF), dt),
        params)
    (y_sample, sample_ret, sample_ssm, sample_ssm_conv, sample_gdn, sample_gdn_conv, sample_mlstm_c,
     sample_mlstm_n, sample_mlstm_m, sample_ffn_conv) = _trunk(
        x_sample, PAST_LEN, state_ret, state_ssm, state_ssm_conv, state_gdn, state_gdn_conv,
        state_mlstm_c, state_mlstm_n, state_mlstm_m, state_ffn_conv, params)
    return (y_prompt, y_sample,
            prompt_ret, prompt_ssm, prompt_ssm_conv, prompt_gdn, prompt_gdn_conv,
            prompt_mlstm_c, prompt_mlstm_n, prompt_mlstm_m, prompt_ffn_conv,
            sample_ret, sample_ssm, sample_ssm_conv, sample_gdn, sample_gdn_conv,
            sample_mlstm_c, sample_mlstm_n, sample_mlstm_m, sample_ffn_conv)
```

```python
import functools

import jax
import jax.numpy as jnp
from jax import lax
from jax.experimental import pallas as pl
from jax.experimental.pallas import tpu as pltpu

F32 = jnp.float32
BF16 = jnp.bfloat16

D_MODEL = 4096
PAST_LEN = 16384
CHUNK = 128
NORM_EPS = 1e-6
ROPE_BASE = 10000.0

RET_HEADS = 16
RET_DK = 256
RET_DV = 512
RET_QK_DIM = RET_HEADS * RET_DK
RET_V_DIM = RET_HEADS * RET_DV

SSM_D_INNER = 8192
SSM_HEAD_DIM = 64
SSM_HEADS = 128
SSM_STATE = 128
SSM_GROUPS = 8
SSM_HPG = SSM_HEADS // SSM_GROUPS
SSM_GROUP_W = SSM_HPG * SSM_HEAD_DIM
SSM_CONV = 4
SSM_CONV_DIM = SSM_D_INNER + 2 * SSM_GROUPS * SSM_STATE

GDN_K_HEADS = 32
GDN_V_HEADS = 64
GDN_REP = GDN_V_HEADS // GDN_K_HEADS
GDN_DK = 128
GDN_DV = 128
GDN_CONV = 4
GDN_KEY_DIM = GDN_K_HEADS * GDN_DK
GDN_VAL_DIM = GDN_V_HEADS * GDN_DV
GDN_CONV_DIM = 2 * GDN_KEY_DIM + GDN_VAL_DIM

ML_HEADS = 8
ML_DQK = 256
ML_DV = 512
ML_QK_DIM = ML_HEADS * ML_DQK
ML_V_DIM = ML_HEADS * ML_DV

D_FF = 11008
FFN_CONV = 3

VMEM_LIMIT_BYTES = 56 * 1024 * 1024
SUBLANES = 8
LANES = 128


def _cparams(*sem):
    return pltpu.CompilerParams(dimension_semantics=sem, vmem_limit_bytes=VMEM_LIMIT_BYTES)


def _bdot(a, b):
    return jnp.dot(a.astype(BF16), b.astype(BF16), preferred_element_type=F32)


def _bdot_nt(a, b):
    return lax.dot_general(a.astype(BF16), b.astype(BF16), (((1,), (1,)), ((), ())), preferred_element_type=F32)


def _bdot_tn(a, b):
    return lax.dot_general(a.astype(BF16), b.astype(BF16), (((0,), (0,)), ((), ())), preferred_element_type=F32)


def _split2(x):
    hi = x.astype(BF16)
    lo = (x - hi.astype(F32)).astype(BF16)
    return hi, lo


def _split3(x):
    hi = x.astype(BF16)
    r = x - hi.astype(F32)
    mid = r.astype(BF16)
    lo = (r - mid.astype(F32)).astype(BF16)
    return hi, mid, lo


def _dot3(a, b):
    ah, al = _split2(a)
    bh, bl = _split2(b)
    d = functools.partial(jnp.dot, preferred_element_type=F32)
    return d(ah, bh) + (d(ah, bl) + d(al, bh))


def _cumsum_rows(x):
    L = x.shape[0]
    tri = (lax.broadcasted_iota(jnp.int32, (L, L), 0) >= lax.broadcasted_iota(jnp.int32, (L, L), 1)).astype(BF16)
    d = functools.partial(jnp.dot, preferred_element_type=F32)
    hi, mid, lo = _split3(x)
    return d(tri, hi) + (d(tri, mid) + d(tri, lo))


def _cumsum_cols(x):
    L = x.shape[1]
    tri = (lax.broadcasted_iota(jnp.int32, (L, L), 0) <= lax.broadcasted_iota(jnp.int32, (L, L), 1)).astype(BF16)
    d = functools.partial(jnp.dot, preferred_element_type=F32)
    hi, mid, lo = _split3(x)
    return d(hi, tri) + (d(mid, tri) + d(lo, tri))


def _softplus(x):
    return jnp.maximum(x, 0.0) + jnp.log1p(jnp.exp(-jnp.abs(x)))


def _silu(x):
    return x * jax.nn.sigmoid(x)


def _rms(x):
    return x * lax.rsqrt(jnp.mean(x * x, axis=-1, keepdims=True) + NORM_EPS)


def _tri_masks(L):
    ri = lax.broadcasted_iota(jnp.int32, (L, L), 0)
    ci = lax.broadcasted_iota(jnp.int32, (L, L), 1)
    return ri, ci


def _rmsnorm_body(x_ref, w_ref, o_ref):
    o_ref[...] = (_rms(x_ref[...]) * w_ref[...]).astype(o_ref.dtype)


def rmsnorm(x, w, out_dtype):
    M, D = x.shape
    tm = min(M, 256)
    return pl.pallas_call(
        _rmsnorm_body,
        out_shape=jax.ShapeDtypeStruct((M, D), out_dtype),
        grid=(M // tm,),
        in_specs=[pl.BlockSpec((tm, D), lambda i: (i, 0)), pl.BlockSpec((1, D), lambda i: (0, 0))],
        out_specs=pl.BlockSpec((tm, D), lambda i: (i, 0)),
        compiler_params=_cparams("parallel"),
        name="rmsnorm",
    )(x, w.reshape(1, D))


def _mm_body(a_ref, b_ref, o_ref):
    o_ref[...] = jnp.dot(a_ref[...], b_ref[...], preferred_element_type=F32)


def _mm_res_body(a_ref, b_ref, r_ref, o_ref):
    o_ref[...] = r_ref[...] + jnp.dot(a_ref[...], b_ref[...], preferred_element_type=F32)


def _mm_tiles(M, K):
    if M <= 128:
        return M, (1024 if K <= 4096 else 512), False
    tm = min(M, 1024)
    if K <= 4096:
        return tm, 512, False
    if K <= 8192:
        return tm, 256, False
    return tm, 256, True


def matmul(a, b, res=None, name="matmul"):
    M, K = a.shape
    N = b.shape[1]
    tm, tn, single = _mm_tiles(M, K)
    tn = min(tn, N)
    a_spec = pl.BlockSpec((tm, K), lambda i, j: (i, 0), **({"pipeline_mode": pl.Buffered(1)} if single else {}))
    in_specs = [a_spec, pl.BlockSpec((K, tn), lambda i, j: (0, j))]
    args = [a, b]
    body = _mm_body
    if res is not None:
        in_specs.append(pl.BlockSpec((tm, tn), lambda i, j: (i, j)))
        args.append(res)
        body = _mm_res_body
    return pl.pallas_call(
        body,
        out_shape=jax.ShapeDtypeStruct((M, N), F32),
        grid=(M // tm, pl.cdiv(N, tn)),
        in_specs=in_specs,
        out_specs=pl.BlockSpec((tm, tn), lambda i, j: (i, j)),
        compiler_params=_cparams("parallel", "parallel"),
        name=name,
    )(*args)


FFN_TN = 256


def _ffn1_seq_body(x_ref, wg_ref, wu_ref, cw_ref, cb_ref, act_ref, tail_ref, carry_ref, gbuf_ref, *, tiles_per_seq):
    i = pl.program_id(0)
    j = pl.program_id(1)
    x = x_ref[...]
    g = jnp.dot(x, wg_ref[...], preferred_element_type=F32)
    u = jnp.dot(x, wu_ref[...], preferred_element_type=F32)
    tm = g.shape[0]
    @pl.when(i % tiles_per_seq == 0)
    def _():
        gbuf_ref[0:SUBLANES, :] = jnp.zeros((SUBLANES, g.shape[1]), F32)

    @pl.when(i % tiles_per_seq != 0)
    def _():
        gbuf_ref[0:SUBLANES, :] = carry_ref[j]

    gbuf_ref[SUBLANES:SUBLANES + tm, :] = g
    w = cw_ref[...]
    y = (gbuf_ref[SUBLANES - 2:SUBLANES - 2 + tm, :] * w[0:1] + gbuf_ref[SUBLANES - 1:SUBLANES - 1 + tm, :] * w[1:2]
         + g * w[2:3] + cb_ref[...])
    act_ref[...] = (_silu(y) * u).astype(act_ref.dtype)
    tail = g[tm - SUBLANES:tm, :]
    carry_ref[j] = tail
    tail_ref[0] = tail


def ffn1_seq(x, wg, wu, cw, cb, seq_len):
    M, D = x.shape
    F = wg.shape[1]
    tm = min(seq_len, 1024)
    tn = FFN_TN
    nj = F // tn
    return pl.pallas_call(
        functools.partial(_ffn1_seq_body, tiles_per_seq=seq_len // tm),
        out_shape=(jax.ShapeDtypeStruct((M, F), BF16), jax.ShapeDtypeStruct((M // tm, SUBLANES, F), F32)),
        grid=(M // tm, nj),
        in_specs=[
            pl.BlockSpec((tm, D), lambda i, j: (i, 0)),
            pl.BlockSpec((D, tn), lambda i, j: (0, j)),
            pl.BlockSpec((D, tn), lambda i, j: (0, j)),
            pl.BlockSpec((FFN_CONV, tn), lambda i, j: (0, j)),
            pl.BlockSpec((1, tn), lambda i, j: (0, j)),
        ],
        out_specs=(pl.BlockSpec((tm, tn), lambda i, j: (i, j)), pl.BlockSpec((1, SUBLANES, tn), lambda i, j: (i, 0, j))),
        scratch_shapes=[pltpu.VMEM((nj, SUBLANES, tn), F32), pltpu.VMEM((tm + SUBLANES, tn), F32)],
        compiler_params=_cparams("arbitrary", "arbitrary"),
        name="ffn1_seq",
    )(x, wg, wu, cw, cb.reshape(1, F))


def _ffn1_step_body(x_ref, wg_ref, wu_ref, cw_ref, cb_ref, s0_ref, s1_ref, act_ref, g_ref):
    x = x_ref[...]
    g = jnp.dot(x, wg_ref[...], preferred_element_type=F32)
    u = jnp.dot(x, wu_ref[...], preferred_element_type=F32)
    w = cw_ref[...]
    y = s0_ref[...] * w[0:1] + s1_ref[...] * w[1:2] + g * w[2:3] + cb_ref[...]
    act_ref[...] = (_silu(y) * u).astype(act_ref.dtype)
    g_ref[...] = g


def ffn1_step(x, wg, wu, cw, cb, s0, s1):
    M, D = x.shape
    F = wg.shape[1]
    tn = FFN_TN
    return pl.pallas_call(
        _ffn1_step_body,
        out_shape=(jax.ShapeDtypeStruct((M, F), BF16), jax.ShapeDtypeStruct((M, F), F32)),
        grid=(F // tn,),
        in_specs=[
            pl.BlockSpec((M, D), lambda j: (0, 0)),
            pl.BlockSpec((D, tn), lambda j: (0, j)),
            pl.BlockSpec((D, tn), lambda j: (0, j)),
            pl.BlockSpec((FFN_CONV, tn), lambda j: (0, j)),
            pl.BlockSpec((1, tn), lambda j: (0, j)),
            pl.BlockSpec((M, tn), lambda j: (0, j)),
            pl.BlockSpec((M, tn), lambda j: (0, j)),
        ],
        out_specs=(pl.BlockSpec((M, tn), lambda j: (0, j)), pl.BlockSpec((M, tn), lambda j: (0, j))),
        compiler_params=_cparams("parallel"),
        name="ffn1_step",
    )(x, wg, wu, cw, cb.reshape(1, F), s0, s1)


CONV_TC = 1024


def _conv_seq_body(x_ref, w_ref, b_ref, o_ref, buf_ref, *, width):
    t = pl.program_id(2)
    tt = x_ref.shape[0]

    @pl.when(t == 0)
    def _():
        buf_ref[0:SUBLANES, :] = jnp.zeros((SUBLANES, buf_ref.shape[1]), F32)

    x = x_ref[...]
    buf_ref[SUBLANES:SUBLANES + tt, :] = x
    w = w_ref[...]
    y = x * w[width - 1:width]
    for j in range(1, width):
        y = y + buf_ref[SUBLANES - j:SUBLANES - j + tt, :] * w[width - 1 - j:width - j]
    o_ref[...] = _silu(y + b_ref[...])
    buf_ref[0:SUBLANES, :] = x[tt - SUBLANES:tt, :]


def conv_seq(proj, col0, w, b, nseq, seq_len):
    width, C = w.shape
    tc = CONV_TC
    tt = min(seq_len, 512)
    nt = seq_len // tt
    c0 = col0 // tc
    return pl.pallas_call(
        functools.partial(_conv_seq_body, width=width),
        out_shape=jax.ShapeDtypeStruct((nseq * seq_len, C), F32),
        grid=(nseq, C // tc, nt),
        in_specs=[
            pl.BlockSpec((tt, tc), lambda s, c, t: (s * nt + t, c0 + c)),
            pl.BlockSpec((width, tc), lambda s, c, t: (0, c)),
            pl.BlockSpec((1, tc), lambda s, c, t: (0, c)),
        ],
        out_specs=pl.BlockSpec((tt, tc), lambda s, c, t: (s * nt + t, c)),
        scratch_shapes=[pltpu.VMEM((tt + SUBLANES, tc), F32)],
        compiler_params=_cparams("parallel", "parallel", "arbitrary"),
        name="conv_seq",
    )(proj, w, b.reshape(1, C))


def _conv_step_body(x_ref, s0_ref, s1_ref, s2_ref, w_ref, b_ref, o_ref):
    w = w_ref[...]
    y = s0_ref[...] * w[0:1] + s1_ref[...] * w[1:2] + s2_ref[...] * w[2:3] + x_ref[...] * w[3:4] + b_ref[...]
    o_ref[...] = _silu(y)


def conv_step(proj, col0, w, b, hist):
    width, C = w.shape
    M = proj.shape[0]
    tc = CONV_TC
    c0 = col0 // tc
    spec = pl.BlockSpec((M, tc), lambda c: (0, c))
    return pl.pallas_call(
        _conv_step_body,
        out_shape=jax.ShapeDtypeStruct((M, C), F32),
        grid=(C // tc,),
        in_specs=[pl.BlockSpec((M, tc), lambda c: (0, c0 + c)), spec, spec, spec,
                  pl.BlockSpec((width, tc), lambda c: (0, c)), pl.BlockSpec((1, tc), lambda c: (0, c))],
        out_specs=spec,
        compiler_params=_cparams("parallel"),
        name="conv_step",
    )(proj, hist[:, 0], hist[:, 1], hist[:, 2], w, b.reshape(1, C))


def _ret_log_gamma(h):
    return jnp.log(1.0 - jnp.exp2(-5.0 - jnp.full((1, 1), h, jnp.int32).astype(F32)))


def _rope(x, cos, sin):
    half = x.shape[-1] // 2
    x1, x2 = x[:, :half], x[:, half:]
    return jnp.concatenate([x1 * cos - x2 * sin, x1 * sin + x2 * cos], axis=-1)


def _ret_chunk_body(q_ref, k_ref, v_ref, g_ref, cos_ref, sin_ref, o_ref, s_ref):
    h = pl.program_id(1)
    c = pl.program_id(2)
    L = q_ref.shape[0]

    @pl.when(c == 0)
    def _():
        s_ref[...] = jnp.zeros(s_ref.shape, F32)

    lg = _ret_log_gamma(h)
    ri, ci = _tri_masks(L)
    rel = (ri - ci).astype(F32)
    decay_in = jnp.where(rel >= 0, jnp.exp(lg * jnp.maximum(rel, 0.0)), 0.0)
    idx = lax.broadcasted_iota(jnp.int32, (L, 1), 0).astype(F32)
    decay_q = jnp.exp((idx + 1.0) * lg)
    decay_k = jnp.exp((L - 1.0 - idx) * lg)
    decay_s = jnp.exp(L * lg)
    cos, sin = cos_ref[...], sin_ref[...]
    q = _rope(q_ref[...], cos, sin)
    k = _rope(k_ref[...], cos, sin) * (RET_DK ** -0.5)
    v = v_ref[...]
    s = s_ref[0, 0]
    sc = _bdot_nt(q, k) * decay_in
    o = _bdot(sc, v) + _bdot(q, s) * decay_q
    s_ref[0, 0] = decay_s * s + _bdot_tn(k * decay_k, v)
    o_ref[...] = (_silu(g_ref[...]) * _rms(o)).astype(o_ref.dtype)


def _rope_tables(pos):
    half = RET_DK // 2
    inv_freq = 1.0 / (ROPE_BASE ** jnp.linspace(0.0, 1.0, half, dtype=F32))
    ang = pos.astype(F32)[:, None] * inv_freq[None, :]
    return jnp.cos(ang), jnp.sin(ang)


def ret_prompt(proj, nseq, seq_len):
    L = min(CHUNK, seq_len)
    nc = seq_len // L
    M = nseq * seq_len
    cos, sin = _rope_tables(jnp.arange(seq_len))
    kq, kv, kg = RET_QK_DIM // RET_DK, (2 * RET_QK_DIM) // RET_DV, (2 * RET_QK_DIM + RET_V_DIM) // RET_DV
    return pl.pallas_call(
        _ret_chunk_body,
        out_shape=(jax.ShapeDtypeStruct((M, RET_V_DIM), BF16),
                   jax.ShapeDtypeStruct((nseq, RET_HEADS, RET_DK, RET_DV), F32)),
        grid=(nseq, RET_HEADS, nc),
        in_specs=[
            pl.BlockSpec((L, RET_DK), lambda b, h, c: (b * nc + c, h)),
            pl.BlockSpec((L, RET_DK), lambda b, h, c: (b * nc + c, kq + h)),
            pl.BlockSpec((L, RET_DV), lambda b, h, c: (b * nc + c, kv + h)),
            pl.BlockSpec((L, RET_DV), lambda b, h, c: (b * nc + c, kg + h)),
            pl.BlockSpec((L, RET_DK // 2), lambda b, h, c: (c, 0)),
            pl.BlockSpec((L, RET_DK // 2), lambda b, h, c: (c, 0)),
        ],
        out_specs=(pl.BlockSpec((L, RET_DV), lambda b, h, c: (b * nc + c, h)),
                   pl.BlockSpec((1, 1, RET_DK, RET_DV), lambda b, h, c: (b, h, 0, 0))),
        compiler_params=_cparams("parallel", "parallel", "arbitrary"),
        name="ret_chunk",
    )(proj, proj, proj, proj, cos, sin)


STEP_TB = 8


def _ret_step_body(q_ref, k_ref, v_ref, g_ref, cos_ref, sin_ref, s_ref, o_ref, so_ref):
    h = pl.program_id(1)
    tb = q_ref.shape[0]
    gamma = jnp.exp(_ret_log_gamma(h))
    cos, sin = cos_ref[...], sin_ref[...]
    q = _rope(q_ref[...], cos, sin)
    k = _rope(k_ref[...], cos, sin) * (RET_DK ** -0.5)
    v = v_ref[...]
    g = g_ref[...]
    sc = jnp.sum(q * k, axis=-1, keepdims=True)
    qt = q.T
    kt = k.T
    for b in range(tb):
        s = s_ref[b, 0]
        vb = v[b:b + 1, :]
        qs = jnp.sum(s * qt[:, b:b + 1], axis=0, keepdims=True)
        o = sc[b:b + 1, :] * vb + qs * gamma
        so_ref[b, 0] = gamma * s + kt[:, b:b + 1] * vb
        o_ref[b:b + 1, :] = (_silu(g[b:b + 1, :]) * _rms(o)).astype(o_ref.dtype)


def ret_step(proj, state, pos):
    M = proj.shape[0]
    tb = STEP_TB
    cos, sin = _rope_tables(jnp.full((1,), pos))
    kq, kv, kg = RET_QK_DIM // RET_DK, (2 * RET_QK_DIM) // RET_DV, (2 * RET_QK_DIM + RET_V_DIM) // RET_DV
    s_spec = pl.BlockSpec((tb, 1, RET_DK, RET_DV), lambda i, h: (i, h, 0, 0))
    return pl.pallas_call(
        _ret_step_body,
        out_shape=(jax.ShapeDtypeStruct((M, RET_V_DIM), BF16), jax.ShapeDtypeStruct(state.shape, F32)),
        grid=(M // tb, RET_HEADS),
        in_specs=[
            pl.BlockSpec((tb, RET_DK), lambda i, h: (i, h)),
            pl.BlockSpec((tb, RET_DK), lambda i, h: (i, kq + h)),
            pl.BlockSpec((tb, RET_DV), lambda i, h: (i, kv + h)),
            pl.BlockSpec((tb, RET_DV), lambda i, h: (i, kg + h)),
            pl.BlockSpec((1, RET_DK // 2), lambda i, h: (0, 0)),
            pl.BlockSpec((1, RET_DK // 2), lambda i, h: (0, 0)),
            s_spec,
        ],
        out_specs=(pl.BlockSpec((tb, RET_DV), lambda i, h: (i, h)), s_spec),
        compiler_params=_cparams("parallel", "parallel"),
        name="ret_step",
    )(proj, proj, proj, proj, cos, sin, state)


def _ssm_chunk_body(x_ref, b_ref, c_ref, z_ref, dtr_ref, dtc_ref, pr_ref, pc_ref, dskip_ref, nw_ref, y_ref, s_ref):
    c = pl.program_id(2)
    L = x_ref.shape[0]
    P2 = 2 * SSM_HEAD_DIM

    @pl.when(c == 0)
    def _():
        s_ref[...] = jnp.zeros(s_ref.shape, F32)

    pr = pr_ref[0]
    pc = pc_ref[0]
    dt_r = _softplus(dtr_ref[0] + pr[0:1, :])
    dt_c = _softplus(dtc_ref[0] + pc[:, 0:1])
    acs_r = _cumsum_rows(dt_r * (-jnp.exp(pr[1:2, :])))
    acs_c = _cumsum_cols(dt_c * (-jnp.exp(pc[:, 1:2])))
    bm = b_ref[...]
    cm = c_ref[...]
    x = x_ref[...]
    cb = _bdot_nt(cm, bm)
    ri, ci = _tri_masks(L)
    incl = ri >= ci
    lo_lane = lax.broadcasted_iota(jnp.int32, (L, P2), 1) < SSM_HEAD_DIM
    lo_row = lax.broadcasted_iota(jnp.int32, (P2, 1), 0) < SSM_HEAD_DIM
    ys = []
    for p in range(SSM_HPG // 2):
        h0, h1 = 2 * p, 2 * p + 1
        sc = []
        for h in (h0, h1):
            seg = acs_r[:, h:h + 1] - acs_c[h:h + 1, :]
            sc.append(cb * jnp.exp(jnp.where(incl, seg, -jnp.inf)) * dt_c[h:h + 1, :])
        xp = x[:, p * P2:(p + 1) * P2]
        x2 = jnp.concatenate([jnp.where(lo_lane, xp, 0.0), jnp.where(lo_lane, 0.0, xp)], axis=0)
        y = _bdot(jnp.concatenate(sc, axis=1), x2)
        st = s_ref[0, h0:h0 + 2].reshape(P2, SSM_STATE)
        ea = jnp.where(lo_lane, jnp.exp(acs_r[:, h0:h0 + 1]), jnp.exp(acs_r[:, h1:h1 + 1]))
        y = y + _bdot_nt(cm, st) * ea
        a_end0 = acs_r[L - 1:L, h0:h0 + 1]
        a_end1 = acs_r[L - 1:L, h1:h1 + 1]
        w_end = jnp.where(lo_lane, jnp.exp(a_end0 - acs_r[:, h0:h0 + 1]) * dt_r[:, h0:h0 + 1],
                          jnp.exp(a_end1 - acs_r[:, h1:h1 + 1]) * dt_r[:, h1:h1 + 1])
        st_new = jnp.where(lo_row, jnp.exp(a_end0), jnp.exp(a_end1)) * st + _bdot_tn(xp * w_end, bm)
        s_ref[0, h0:h0 + 2] = st_new.reshape(2, SSM_HEAD_DIM, SSM_STATE)
        ys.append(y)
    y = jnp.concatenate(ys, axis=1) + dskip_ref[...] * x
    y = y * _silu(z_ref[...])
    y_ref[...] = (_rms(y) * nw_ref[...]).astype(y_ref.dtype)


def _ssm_params(dt_bias, a_log):
    pr = jnp.stack([dt_bias.reshape(SSM_GROUPS, SSM_HPG), a_log.reshape(SSM_GROUPS, SSM_HPG)], axis=1)
    return pr, jnp.swapaxes(pr, 1, 2)


def ssm_prompt(proj, xa, dt_bias, a_log, d_skip, norm_w, nseq, seq_len):
    L = min(CHUNK, seq_len)
    nc = seq_len // L
    M = nseq * seq_len
    W = SSM_GROUP_W
    dt_raw = proj[:, SSM_D_INNER + SSM_CONV_DIM:]
    dtr = jnp.transpose(dt_raw.reshape(M, SSM_GROUPS, SSM_HPG), (1, 0, 2))
    dtc = jnp.swapaxes(dtr, 1, 2)
    pr, pc = _ssm_params(dt_bias, a_log)
    dskip = jnp.repeat(d_skip, SSM_HEAD_DIM).reshape(1, SSM_D_INNER)
    kb = SSM_D_INNER // SSM_STATE
    return pl.pallas_call(
        _ssm_chunk_body,
        out_shape=(jax.ShapeDtypeStruct((M, SSM_D_INNER), BF16),
                   jax.ShapeDtypeStruct((nseq, SSM_HEADS, SSM_HEAD_DIM, SSM_STATE), F32)),
        grid=(nseq, SSM_GROUPS, nc),
        in_specs=[
            pl.BlockSpec((L, W), lambda b, g, c: (b * nc + c, g)),
            pl.BlockSpec((L, SSM_STATE), lambda b, g, c: (b * nc + c, kb + g)),
            pl.BlockSpec((L, SSM_STATE), lambda b, g, c: (b * nc + c, kb + SSM_GROUPS + g)),
            pl.BlockSpec((L, W), lambda b, g, c: (b * nc + c, g)),
            pl.BlockSpec((1, L, SSM_HPG), lambda b, g, c: (g, b * nc + c, 0)),
            pl.BlockSpec((1, SSM_HPG, L), lambda b, g, c: (g, 0, b * nc + c)),
            pl.BlockSpec((1, 2, SSM_HPG), lambda b, g, c: (g, 0, 0)),
            pl.BlockSpec((1, SSM_HPG, 2), lambda b, g, c: (g, 0, 0)),
            pl.BlockSpec((1, W), lambda b, g, c: (0, g)),
            pl.BlockSpec((1, W), lambda b, g, c: (0, g)),
        ],
        out_specs=(pl.BlockSpec((L, W), lambda b, g, c: (b * nc + c, g)),
                   pl.BlockSpec((1, SSM_HPG, SSM_HEAD_DIM, SSM_STATE), lambda b, g, c: (b, g, 0, 0))),
        compiler_params=_cparams("parallel", "parallel", "arbitrary"),
        name="ssm_chunk",
    )(xa, xa, xa, proj, dtr, dtc, pr, pc, dskip, norm_w.reshape(1, SSM_D_INNER))


def _ssm_step_body(x_ref, b_ref, c_ref, z_ref, dtx_ref, bias_ref, alog_ref, dskip_ref, nw_ref, s_ref, y_ref, so_ref):
    tb = x_ref.shape[0]
    W = SSM_GROUP_W
    x = x_ref[...]
    bm = b_ref[...]
    cm = c_ref[...]
    dt = _softplus(dtx_ref[...] + bias_ref[...])
    da = jnp.exp(dt * (-jnp.exp(alog_ref[...])))
    cb = jnp.sum(cm * bm, axis=-1, keepdims=True)
    xdt = (x * dt).T
    dat = da.T
    rows = lax.broadcasted_iota(jnp.int32, (tb, W), 0)
    cs = jnp.zeros((tb, W), F32)
    for b in range(tb):
        st = s_ref[b].reshape(W, SSM_STATE)
        cs = jnp.where(rows == b, _bdot_nt(cm, st), cs)
        st_new = dat[:, b:b + 1] * st + xdt[:, b:b + 1] * bm[b:b + 1, :]
        so_ref[b] = st_new.reshape(SSM_HPG, SSM_HEAD_DIM, SSM_STATE)
    y = cb * dt * x + cs * da + dskip_ref[...] * x
    y = y * _silu(z_ref[...])
    y_ref[...] = (_rms(y) * nw_ref[...]).astype(y_ref.dtype)


def ssm_step(proj, xa, state, dt_bias, a_log, d_skip, norm_w):
    M = proj.shape[0]
    tb = STEP_TB
    W = SSM_GROUP_W
    rep = lambda a: jnp.repeat(a, SSM_HEAD_DIM, axis=-1)
    dtx = rep(proj[:, SSM_D_INNER + SSM_CONV_DIM:])
    kb = SSM_D_INNER // SSM_STATE
    s_spec = pl.BlockSpec((tb, SSM_HPG, SSM_HEAD_DIM, SSM_STATE), lambda g, i: (i, g, 0, 0))
    row = pl.BlockSpec((1, W), lambda g, i: (0, g))
    tile = pl.BlockSpec((tb, W), lambda g, i: (i, g))
    return pl.pallas_call(
        _ssm_step_body,
        out_shape=(jax.ShapeDtypeStruct((M, SSM_D_INNER), BF16), jax.ShapeDtypeStruct(state.shape, F32)),
        grid=(SSM_GROUPS, M // tb),
        in_specs=[
            tile,
            pl.BlockSpec((tb, SSM_STATE), lambda g, i: (i, kb + g)),
            pl.BlockSpec((tb, SSM_STATE), lambda g, i: (i, kb + SSM_GROUPS + g)),
            tile, tile, row, row, row, row, s_spec,
        ],
        out_specs=(tile, s_spec),
        compiler_params=_cparams("parallel", "parallel"),
        name="ssm_step",
    )(xa, xa, xa, proj, dtx, rep(dt_bias).reshape(1, -1), rep(a_log).reshape(1, -1), rep(d_skip).reshape(1, -1),
      norm_w.reshape(1, -1), state)


def _l2n(x):
    return x * lax.rsqrt(jnp.sum(x * x, axis=-1, keepdims=True) + NORM_EPS)


def _unit_lower_inverse(a, ri, ci):
    L = a.shape[0]
    eye = (ri == ci).astype(F32)
    n = jnp.where((ri >> 3) == (ci >> 3), -a, 0.0)
    n2 = _dot3(n, n)
    n4 = _dot3(n2, n2)
    t = eye + n
    t = t + _dot3(t, n2)
    t = t + _dot3(t, n4)
    sh = 3
    while (1 << sh) < L:
        same_pair = (ri >> (sh + 1)) == (ci >> (sh + 1))
        off = jnp.where(same_pair, jnp.where((ri >> sh) != (ci >> sh), a, 0.0), 0.0)
        t = t - _dot3(t, _dot3(off, t))
        sh += 1
    return t


def _gdn_chunk_body(q_ref, k_ref, v_ref, z_ref, gr_ref, gc_ref, pr_ref, pc_ref, nw_ref, o_ref, s_ref):
    c = pl.program_id(2)
    L = q_ref.shape[0]

    @pl.when(c == 0)
    def _():
        s_ref[...] = jnp.zeros(s_ref.shape, F32)

    q = _l2n(q_ref[...]) * (GDN_DK ** -0.5)
    k = _l2n(k_ref[...])
    v = v_ref[...]
    z = z_ref[...]
    gr = gr_ref[0]
    gc = gc_ref[0]
    pr = pr_ref[0]
    pc = pc_ref[0]
    beta = jax.nn.sigmoid(gr[:, 0:GDN_REP])
    gt_r = _cumsum_rows(-jnp.exp(pr[0:1, :]) * _softplus(gr[:, GDN_REP:] + pr[1:2, :]))
    gt_c = _cumsum_cols(-jnp.exp(pc[:, 0:1]) * _softplus(gc[GDN_REP:, :] + pc[:, 1:2]))
    kk = _bdot_nt(k, k)
    qk = _bdot_nt(q, k)
    ri, ci = _tri_masks(L)
    nw = nw_ref[...]
    for e in range(GDN_REP):
        gcol = gt_r[:, e:e + 1]
        seg = gcol - gt_c[e:e + 1, :]
        bcol = beta[:, e:e + 1]
        a_mat = kk * jnp.exp(jnp.where(ri > ci, seg, -jnp.inf)) * bcol
        s = s_ref[0, e]
        eg = jnp.exp(gcol)
        rhs = bcol * (v[:, e * GDN_DV:(e + 1) * GDN_DV] - eg * _bdot(k, s))
        delta = _dot3(_unit_lower_inverse(a_mat, ri, ci), rhs)
        o = eg * _bdot(q, s) + _bdot(qk * jnp.exp(jnp.where(ri >= ci, seg, -jnp.inf)), delta)
        g_end = gt_r[L - 1:L, e:e + 1]
        s_ref[0, e] = jnp.exp(g_end) * s + _bdot_tn(k * jnp.exp(g_end - gcol), delta)
        ze = z[:, e * GDN_DV:(e + 1) * GDN_DV]
        o_ref[:, e * GDN_DV:(e + 1) * GDN_DV] = (_rms(o) * nw * _silu(ze)).astype(o_ref.dtype)


def _gdn_gate_inputs(proj):
    M = proj.shape[0]
    tail = proj[:, GDN_CONV_DIM + GDN_VAL_DIM:]
    braw = tail[:, :GDN_V_HEADS].reshape(M, GDN_K_HEADS, GDN_REP)
    araw = tail[:, GDN_V_HEADS:].reshape(M, GDN_K_HEADS, GDN_REP)
    gr = jnp.transpose(jnp.concatenate([braw, araw], axis=-1), (1, 0, 2))
    return gr, jnp.swapaxes(gr, 1, 2)


def _gdn_params(a_log, dt_bias):
    pr = jnp.stack([a_log.reshape(GDN_K_HEADS, GDN_REP), dt_bias.reshape(GDN_K_HEADS, GDN_REP)], axis=1)
    return pr, jnp.swapaxes(pr, 1, 2)


def gdn_prompt(proj, qkv, a_log, dt_bias, norm_w, nseq, seq_len):
    L = min(CHUNK, seq_len)
    nc = seq_len // L
    M = nseq * seq_len
    VW = GDN_REP * GDN_DV
    gr, gc = _gdn_gate_inputs(proj)
    pr, pc = _gdn_params(a_log, dt_bias)
    kk, kv, kz = GDN_KEY_DIM // GDN_DK, (2 * GDN_KEY_DIM) // VW, GDN_CONV_DIM // VW
    return pl.pallas_call(
        _gdn_chunk_body,
        out_shape=(jax.ShapeDtypeStruct((M, GDN_VAL_DIM), BF16),
                   jax.ShapeDtypeStruct((nseq, GDN_V_HEADS, GDN_DK, GDN_DV), F32)),
        grid=(nseq, GDN_K_HEADS, nc),
        in_specs=[
            pl.BlockSpec((L, GDN_DK), lambda b, h, c: (b * nc + c, h)),
            pl.BlockSpec((L, GDN_DK), lambda b, h, c: (b * nc + c, kk + h)),
            pl.BlockSpec((L, VW), lambda b, h, c: (b * nc + c, kv + h)),
            pl.BlockSpec((L, VW), lambda b, h, c: (b * nc + c, kz + h)),
            pl.BlockSpec((1, L, 2 * GDN_REP), lambda b, h, c: (h, b * nc + c, 0)),
            pl.BlockSpec((1, 2 * GDN_REP, L), lambda b, h, c: (h, 0, b * nc + c)),
            pl.BlockSpec((1, 2, GDN_REP), lambda b, h, c: (h, 0, 0)),
            pl.BlockSpec((1, GDN_REP, 2), lambda b, h, c: (h, 0, 0)),
            pl.BlockSpec((1, GDN_DV), lambda b, h, c: (0, 0)),
        ],
        out_specs=(pl.BlockSpec((L, VW), lambda b, h, c: (b * nc + c, h)),
                   pl.BlockSpec((1, GDN_REP, GDN_DK, GDN_DV), lambda b, h, c: (b, h, 0, 0))),
        compiler_params=_cparams("parallel", "parallel", "arbitrary"),
        name="gdn_chunk",
    )(qkv, qkv, qkv, proj, gr, gc, pr, pc, norm_w.reshape(1, GDN_DV))


def _gdn_step_body(q_ref, k_ref, v_ref, z_ref, gr_ref, pr_ref, nw_ref, s_ref, o_ref, so_ref):
    tb = q_ref.shape[0]
    q = _l2n(q_ref[...]) * (GDN_DK ** -0.5)
    k = _l2n(k_ref[...])
    v = v_ref[...]
    z = z_ref[...]
    gr = gr_ref[0]
    pr = pr_ref[0]
    beta = jax.nn.sigmoid(gr[:, 0:GDN_REP])
    eg_all = jnp.exp(-jnp.exp(pr[0:1, :]) * _softplus(gr[:, GDN_REP:] + pr[1:2, :]))
    qk = jnp.sum(q * k, axis=-1, keepdims=True)
    qt = q.T
    kt = k.T
    nw = nw_ref[...]
    for b in range(tb):
        kcol = kt[:, b:b + 1]
        qcol = qt[:, b:b + 1]
        for e in range(GDN_REP):
            s = s_ref[b, e]
            eg = eg_all[b:b + 1, e:e + 1]
            ks = jnp.sum(s * kcol, axis=0, keepdims=True)
            qs = jnp.sum(s * qcol, axis=0, keepdims=True)
            delta = beta[b:b + 1, e:e + 1] * (v[b:b + 1, e * GDN_DV:(e + 1) * GDN_DV] - eg * ks)
            o = eg * qs + qk[b:b + 1, :] * delta
            so_ref[b, e] = eg * s + kcol * delta
            ze = z[b:b + 1, e * GDN_DV:(e + 1) * GDN_DV]
            o_ref[b:b + 1, e * GDN_DV:(e + 1) * GDN_DV] = (_rms(o) * nw * _silu(ze)).astype(o_ref.dtype)


def gdn_step(proj, qkv, state, a_log, dt_bias, norm_w):
    M = proj.shape[0]
    tb = STEP_TB
    VW = GDN_REP * GDN_DV
    gr, _ = _gdn_gate_inputs(proj)
    pr, _ = _gdn_params(a_log, dt_bias)
    kk, kv, kz = GDN_KEY_DIM // GDN_DK, (2 * GDN_KEY_DIM) // VW, GDN_CONV_DIM // VW
    s_spec = pl.BlockSpec((tb, GDN_REP, GDN_DK, GDN_DV), lambda i, h: (i, h, 0, 0))
    return pl.pallas_call(
        _gdn_step_body,
        out_shape=(jax.ShapeDtypeStruct((M, GDN_VAL_DIM), BF16), jax.ShapeDtypeStruct(state.shape, F32)),
        grid=(M // tb, GDN_K_HEADS),
        in_specs=[
            pl.BlockSpec((tb, GDN_DK), lambda i, h: (i, h)),
            pl.BlockSpec((tb, GDN_DK), lambda i, h: (i, kk + h)),
            pl.BlockSpec((tb, VW), lambda i, h: (i, kv + h)),
            pl.BlockSpec((tb, VW), lambda i, h: (i, kz + h)),
            pl.BlockSpec((1, tb, 2 * GDN_REP), lambda i, h: (h, i, 0)),
            pl.BlockSpec((1, 2, GDN_REP), lambda i, h: (h, 0, 0)),
            pl.BlockSpec((1, GDN_DV), lambda i, h: (0, 0)),
            s_spec,
        ],
        out_specs=(pl.BlockSpec((tb, VW), lambda i, h: (i, h)), s_spec),
        compiler_params=_cparams("parallel", "parallel"),
        name="gdn_step",
    )(qkv, qkv, qkv, proj, gr, pr, norm_w.reshape(1, GDN_DV), state)


def _log_sigmoid(x):
    return -_softplus(-x)


def _ml_chunk_body(q_ref, k_ref, v_ref, og_ref, gr_ref, gc_ref, pr_ref, pc_ref, nw_ref, o_ref, c_ref, n_ref, m_ref):
    ch = pl.program_id(2)
    L = q_ref.shape[0]

    @pl.when(ch == 0)
    def _():
        c_ref[...] = jnp.zeros(c_ref.shape, F32)
        n_ref[...] = jnp.zeros(n_ref.shape, F32)
        m_ref[...] = jnp.zeros(m_ref.shape, F32)

    q = q_ref[...]
    k = k_ref[...] * (ML_DQK ** -0.5)
    v = v_ref[...]
    gr = gr_ref[0] + pr_ref[0]
    gc = gc_ref[0] + pc_ref[0]
    it_c = gr[:, 0:1]
    it_r = gc[0:1, :]
    ft_c = _cumsum_rows(_log_sigmoid(gr[:, 1:2]))
    ft_r = _cumsum_cols(_log_sigmoid(gc[1:2, :]))
    c_st = c_ref[0, 0]
    n_st = n_ref[0, 0]
    m_st = m_ref[0, 0][:, 0:1]
    ri, ci = _tri_masks(L)
    dmat = jnp.where(ri >= ci, ft_c - ft_r + it_r, -jnp.inf)
    inter = ft_c + m_st
    mi = jnp.maximum(inter, jnp.max(dmat, axis=-1, keepdims=True))
    wmat = jnp.exp(dmat - mi)
    winter = jnp.exp(inter - mi)
    sc = _bdot_nt(q, k) * wmat
    num = winter * _bdot(q, c_st) + _bdot(sc, v)
    den = winter * jnp.sum(q * n_st, axis=-1, keepdims=True) + jnp.sum(sc, axis=-1, keepdims=True)
    hh = num / jnp.maximum(jnp.abs(den), jnp.exp(-mi))
    m_new = mi[L - 1:L, :]
    f_end = ft_c[L - 1:L, :]
    wk = jnp.exp(f_end - ft_c + it_c - m_new)
    keep = jnp.exp(f_end + m_st - m_new)
    kw = k * wk
    c_ref[0, 0] = keep * c_st + _bdot_tn(kw, v)
    n_ref[0, 0] = keep * n_st + jnp.sum(kw, axis=0, keepdims=True)
    m_ref[0, 0] = jnp.broadcast_to(m_new, (1, LANES))
    o_ref[...] = (_rms(hh) * nw_ref[...] * jax.nn.sigmoid(og_ref[...])).astype(o_ref.dtype)


def _ml_gate_inputs(proj, gate_b):
    M = proj.shape[0]
    tail = proj[:, 2 * ML_QK_DIM + 2 * ML_V_DIM:]
    gr = jnp.transpose(tail.reshape(M, 2, ML_HEADS), (2, 0, 1))
    pr = jnp.transpose(gate_b.reshape(2, ML_HEADS), (1, 0)).reshape(ML_HEADS, 1, 2)
    return gr, jnp.swapaxes(gr, 1, 2), pr, jnp.swapaxes(pr, 1, 2)


def ml_prompt(proj, gate_b, norm_w, nseq, seq_len):
    L = min(CHUNK, seq_len)
    nc = seq_len // L
    M = nseq * seq_len
    gr, gc, pr, pc = _ml_gate_inputs(proj, gate_b)
    kk, kv, ko = ML_QK_DIM // ML_DQK, (2 * ML_QK_DIM) // ML_DV, (2 * ML_QK_DIM + ML_V_DIM) // ML_DV
    return pl.pallas_call(
        _ml_chunk_body,
        out_shape=(jax.ShapeDtypeStruct((M, ML_V_DIM), BF16),
                   jax.ShapeDtypeStruct((nseq, ML_HEADS, ML_DQK, ML_DV), F32),
                   jax.ShapeDtypeStruct((nseq, ML_HEADS, 1, ML_DQK), F32),
                   jax.ShapeDtypeStruct((nseq, ML_HEADS, 1, LANES), F32)),
        grid=(nseq, ML_HEADS, nc),
        in_specs=[
            pl.BlockSpec((L, ML_DQK), lambda b, h, c: (b * nc + c, h)),
            pl.BlockSpec((L, ML_DQK), lambda b, h, c: (b * nc + c, kk + h)),
            pl.BlockSpec((L, ML_DV), lambda b, h, c: (b * nc + c, kv + h)),
            pl.BlockSpec((L, ML_DV), lambda b, h, c: (b * nc + c, ko + h)),
            pl.BlockSpec((1, L, 2), lambda b, h, c: (h, b * nc + c, 0)),
            pl.BlockSpec((1, 2, L), lambda b, h, c: (h, 0, b * nc + c)),
            pl.BlockSpec((1, 1, 2), lambda b, h, c: (h, 0, 0)),
            pl.BlockSpec((1, 2, 1), lambda b, h, c: (h, 0, 0)),
            pl.BlockSpec((1, ML_DV), lambda b, h, c: (0, h)),
        ],
        out_specs=(pl.BlockSpec((L, ML_DV), lambda b, h, c: (b * nc + c, h)),
                   pl.BlockSpec((1, 1, ML_DQK, ML_DV), lambda b, h, c: (b, h, 0, 0)),
                   pl.BlockSpec((1, 1, 1, ML_DQK), lambda b, h, c: (b, h, 0, 0)),
                   pl.BlockSpec((1, 1, 1, LANES), lambda b, h, c: (b, h, 0, 0))),
        compiler_params=_cparams("parallel", "parallel", "arbitrary"),
        name="ml_chunk",
    )(proj, proj, proj, proj, gr, gc, pr, pc, norm_w.reshape(1, ML_V_DIM))


def _ml_step_body(q_ref, k_ref, v_ref, og_ref, gr_ref, pr_ref, nw_ref, c_ref, n_ref, m_ref, o_ref, co_ref, no_ref, mo_ref):
    tb = q_ref.shape[0]
    q = q_ref[...]
    k = k_ref[...] * (ML_DQK ** -0.5)
    v = v_ref[...]
    og = og_ref[...]
    gr = gr_ref[0] + pr_ref[0]
    it = gr[:, 0:1]
    ft = _log_sigmoid(gr[:, 1:2])
    qk = jnp.sum(q * k, axis=-1, keepdims=True)
    qt = q.T
    kt = k.T
    nw = nw_ref[...]
    for b in range(tb):
        c_st = c_ref[b, 0]
        n_st = n_ref[b, 0]
        m_st = m_ref[b, 0][:, 0:1]
        qb = q[b:b + 1, :]
        kb = k[b:b + 1, :]
        vb = v[b:b + 1, :]
        itb = it[b:b + 1, :]
        ftb = ft[b:b + 1, :]
        inter = ftb + m_st
        mi = jnp.maximum(inter, itb)
        wmat = jnp.exp(itb - mi)
        winter = jnp.exp(inter - mi)
        sc = qk[b:b + 1, :] * wmat
        num = winter * jnp.sum(c_st * qt[:, b:b + 1], axis=0, keepdims=True) + sc * vb
        den = winter * jnp.sum(qb * n_st, axis=-1, keepdims=True) + sc
        hh = num / jnp.maximum(jnp.abs(den), jnp.exp(-mi))
        wk = jnp.exp(itb - mi)
        keep = jnp.exp(ftb + m_st - mi)
        co_ref[b, 0] = keep * c_st + (kt[:, b:b + 1] * wk) * vb
        no_ref[b, 0] = keep * n_st + kb * wk
        mo_ref[b, 0] = jnp.broadcast_to(mi, (1, LANES))
        o_ref[b:b + 1, :] = (_rms(hh) * nw * jax.nn.sigmoid(og[b:b + 1, :])).astype(o_ref.dtype)


def ml_step(proj, c0, n0, m0, gate_b, norm_w):
    M = proj.shape[0]
    tb = STEP_TB
    gr, _, pr, _ = _ml_gate_inputs(proj, gate_b)
    kk, kv, ko = ML_QK_DIM // ML_DQK, (2 * ML_QK_DIM) // ML_DV, (2 * ML_QK_DIM + ML_V_DIM) // ML_DV
    c_spec = pl.BlockSpec((tb, 1, ML_DQK, ML_DV), lambda i, h: (i, h, 0, 0))
    n_spec = pl.BlockSpec((tb, 1, 1, ML_DQK), lambda i, h: (i, h, 0, 0))
    m_spec = pl.BlockSpec((tb, 1, 1, LANES), lambda i, h: (i, h, 0, 0))
    n4 = n0.reshape(M, ML_HEADS, 1, ML_DQK)
    m4 = jnp.broadcast_to(m0.reshape(M, ML_HEADS, 1, 1), (M, ML_HEADS, 1, LANES))
    return pl.pallas_call(
        _ml_step_body,
        out_shape=(jax.ShapeDtypeStruct((M, ML_V_DIM), BF16), jax.ShapeDtypeStruct(c0.shape, F32),
                   jax.ShapeDtypeStruct(n4.shape, F32), jax.ShapeDtypeStruct(m4.shape, F32)),
        grid=(M // tb, ML_HEADS),
        in_specs=[
            pl.BlockSpec((tb, ML_DQK), lambda i, h: (i, h)),
            pl.BlockSpec((tb, ML_DQK), lambda i, h: (i, kk + h)),
            pl.BlockSpec((tb, ML_DV), lambda i, h: (i, kv + h)),
            pl.BlockSpec((tb, ML_DV), lambda i, h: (i, ko + h)),
            pl.BlockSpec((1, tb, 2), lambda i, h: (h, i, 0)),
            pl.BlockSpec((1, 1, 2), lambda i, h: (h, 0, 0)),
            pl.BlockSpec((1, ML_DV), lambda i, h: (0, h)),
            c_spec, n_spec, m_spec,
        ],
        out_specs=(pl.BlockSpec((tb, ML_DV), lambda i, h: (i, h)), c_spec, n_spec, m_spec),
        compiler_params=_cparams("parallel", "parallel"),
        name="ml_step",
    )(proj, proj, proj, proj, gr, pr, norm_w.reshape(1, ML_V_DIM), c0, n4, m4)


def _trunk(x, pos0, states, p, wb):
    nseq, seq_len, D = x.shape
    M = nseq * seq_len
    seq = states is None
    xs = x.reshape(M, D)
    ffn_bufs = []
    out = {}
    for layer in range(4):
        hn = rmsnorm(xs, p['norm_mix'][layer], BF16)
        if layer == 0:
            proj = matmul(hn, wb['w_ret_in'], name="ret_in")
            if seq:
                y, out['ret'] = ret_prompt(proj, nseq, seq_len)
            else:
                y, out['ret'] = ret_step(proj, states['ret'], pos0)
            xs = matmul(y, wb['w_ret_out'], res=xs, name="ret_out")
        elif layer == 1:
            proj = matmul(hn, wb['w_ssm_in'], name="ssm_in")
            raw = proj[:, SSM_D_INNER:SSM_D_INNER + SSM_CONV_DIM].reshape(nseq, seq_len, SSM_CONV_DIM)
            if seq:
                xa = conv_seq(proj, SSM_D_INNER, p['ssm_conv_w'], p['ssm_conv_b'], nseq, seq_len)
                out['ssm_conv'] = raw[:, seq_len - (SSM_CONV - 1):]
                y, out['ssm'] = ssm_prompt(proj, xa, p['ssm_dt_bias'], p['ssm_a_log'], p['ssm_d_skip'],
                                           p['ssm_norm_w'], nseq, seq_len)
            else:
                xa = conv_step(proj, SSM_D_INNER, p['ssm_conv_w'], p['ssm_conv_b'], states['ssm_conv'])
                out['ssm_conv'] = jnp.concatenate([states['ssm_conv'][:, 1:], raw], axis=1)
                y, out['ssm'] = ssm_step(proj, xa, states['ssm'], p['ssm_dt_bias'], p['ssm_a_log'], p['ssm_d_skip'],
                                         p['ssm_norm_w'])
            xs = matmul(y, wb['w_ssm_out'], res=xs, name="ssm_out")
        elif layer == 2:
            proj = matmul(hn, wb['w_gdn_in'], name="gdn_in")
            raw = proj[:, :GDN_CONV_DIM].reshape(nseq, seq_len, GDN_CONV_DIM)
            zero_b = jnp.zeros((GDN_CONV_DIM,), F32)
            if seq:
                qkv = conv_seq(proj, 0, p['gdn_conv_w'], zero_b, nseq, seq_len)
                out['gdn_conv'] = raw[:, seq_len - (GDN_CONV - 1):]
                y, out['gdn'] = gdn_prompt(proj, qkv, p['gdn_a_log'], p['gdn_dt_bias'], p['gdn_norm_w'], nseq, seq_len)
            else:
                qkv = conv_step(proj, 0, p['gdn_conv_w'], zero_b, states['gdn_conv'])
                out['gdn_conv'] = jnp.concatenate([states['gdn_conv'][:, 1:], raw], axis=1)
                y, out['gdn'] = gdn_step(proj, qkv, states['gdn'], p['gdn_a_log'], p['gdn_dt_bias'], p['gdn_norm_w'])
            xs = matmul(y, wb['w_gdn_out'], res=xs, name="gdn_out")
        else:
            proj = matmul(hn, wb['w_ml_in'], name="ml_in")
            if seq:
                y, c, n, m = ml_prompt(proj, p['ml_gate_b'], p['ml_norm_w'], nseq, seq_len)
            else:
                y, c, n, m = ml_step(proj, states['ml_c'], states['ml_n'], states['ml_m'], p['ml_gate_b'],
                                     p['ml_norm_w'])
            out['ml_c'] = c
            out['ml_n'] = n.reshape(nseq, ML_HEADS, ML_DQK)
            out['ml_m'] = m[:, :, 0, 0]
            xs = matmul(y, wb['w_ml_out'], res=xs, name="ml_out")
        hn = rmsnorm(xs, p['norm_ffn'][layer], BF16)
        wg, wu, wd = wb['ffn_w_gate'][layer], wb['ffn_w_up'][layer], wb['ffn_w_down'][layer]
        cw, cbias = p['ffn_conv_w'][layer], p['ffn_conv_b'][layer]
        if seq:
            act, tails = ffn1_seq(hn, wg, wu, cw, cbias, seq_len)
            tails = tails.reshape(nseq, -1, SUBLANES, D_FF)
            ffn_bufs.append(tails[:, -1, SUBLANES - (FFN_CONV - 1):])
        else:
            hist = states['ffn'][layer]
            act, g_raw = ffn1_step(hn, wg, wu, cw, cbias, hist[:, 0], hist[:, 1])
            ffn_bufs.append(jnp.stack([hist[:, 1], g_raw], axis=1))
        xs = matmul(act, wd, res=xs, name="ffn_down")
    y = rmsnorm(xs, p['norm_final'], F32).reshape(nseq, seq_len, D)
    return (y, out['ret'], out['ssm'], out['ssm_conv'], out['gdn'], out['gdn_conv'], out['ml_c'], out['ml_n'],
            out['ml_m'], jnp.stack(ffn_bufs))


def kernel(x_prompt, x_sample, state_ret, state_ssm, state_ssm_conv, state_gdn, state_gdn_conv, state_mlstm_c, state_mlstm_n, state_mlstm_m, state_ffn_conv, norm_mix, norm_ffn, norm_final, w_ret_in, w_ret_out, w_ssm_in, ssm_conv_w, ssm_conv_b, ssm_dt_bias, ssm_a_log, ssm_d_skip, ssm_norm_w, w_ssm_out, w_gdn_in, gdn_conv_w, gdn_a_log, gdn_dt_bias, gdn_norm_w, w_gdn_out, w_ml_in, ml_gate_b, ml_norm_w, w_ml_out, ffn_w_gate, ffn_w_up, ffn_conv_w, ffn_conv_b, ffn_w_down):
    p = {
        'norm_mix': norm_mix, 'norm_ffn': norm_ffn, 'norm_final': norm_final,
        'ssm_conv_w': ssm_conv_w, 'ssm_conv_b': ssm_conv_b, 'ssm_dt_bias': ssm_dt_bias, 'ssm_a_log': ssm_a_log,
        'ssm_d_skip': ssm_d_skip, 'ssm_norm_w': ssm_norm_w,
        'gdn_conv_w': gdn_conv_w, 'gdn_a_log': gdn_a_log, 'gdn_dt_bias': gdn_dt_bias, 'gdn_norm_w': gdn_norm_w,
        'ml_gate_b': ml_gate_b, 'ml_norm_w': ml_norm_w,
        'ffn_conv_w': ffn_conv_w, 'ffn_conv_b': ffn_conv_b,
    }
    wb = {
        'w_ret_in': w_ret_in.astype(BF16), 'w_ret_out': w_ret_out.astype(BF16),
        'w_ssm_in': w_ssm_in.astype(BF16), 'w_ssm_out': w_ssm_out.astype(BF16),
        'w_gdn_in': w_gdn_in.astype(BF16), 'w_gdn_out': w_gdn_out.astype(BF16),
        'w_ml_in': w_ml_in.astype(BF16), 'w_ml_out': w_ml_out.astype(BF16),
        'ffn_w_gate': [ffn_w_gate[l].astype(BF16) for l in range(4)],
        'ffn_w_up': [ffn_w_up[l].astype(BF16) for l in range(4)],
        'ffn_w_down': [ffn_w_down[l].astype(BF16) for l in range(4)],
    }
    prompt = _trunk(x_prompt, 0, None, p, wb)
    states = {
        'ret': state_ret, 'ssm': state_ssm, 'ssm_conv': state_ssm_conv, 'gdn': state_gdn, 'gdn_conv': state_gdn_conv,
        'ml_c': state_mlstm_c, 'ml_n': state_mlstm_n, 'ml_m': state_mlstm_m, 'ffn': state_ffn_conv,
    }
    sample = _trunk(x_sample, PAST_LEN, states, p, wb)
    return (prompt[0], sample[0]) + prompt[1:] + sample[1:]
```

```python
import functools

import jax
import jax.numpy as jnp
from jax import lax
from jax.experimental import pallas as pl
from jax.experimental.pallas import tpu as pltpu

F32 = jnp.float32
BF16 = jnp.bfloat16

D_MODEL = 4096
PAST_LEN = 16384
CHUNK = 128
NORM_EPS = 1e-6
ROPE_BASE = 10000.0

RET_HEADS = 16
RET_DK = 256
RET_DV = 512
RET_QK_DIM = RET_HEADS * RET_DK
RET_V_DIM = RET_HEADS * RET_DV

SSM_D_INNER = 8192
SSM_HEAD_DIM = 64
SSM_HEADS = 128
SSM_STATE = 128
SSM_GROUPS = 8
SSM_HPG = SSM_HEADS // SSM_GROUPS
SSM_GROUP_W = SSM_HPG * SSM_HEAD_DIM
SSM_CONV = 4
SSM_CONV_DIM = SSM_D_INNER + 2 * SSM_GROUPS * SSM_STATE

GDN_K_HEADS = 32
GDN_V_HEADS = 64
GDN_REP = GDN_V_HEADS // GDN_K_HEADS
GDN_DK = 128
GDN_DV = 128
GDN_CONV = 4
GDN_KEY_DIM = GDN_K_HEADS * GDN_DK
GDN_VAL_DIM = GDN_V_HEADS * GDN_DV
GDN_CONV_DIM = 2 * GDN_KEY_DIM + GDN_VAL_DIM

ML_HEADS = 8
ML_DQK = 256
ML_DV = 512
ML_QK_DIM = ML_HEADS * ML_DQK
ML_V_DIM = ML_HEADS * ML_DV

D_FF = 11008
FFN_CONV = 3

VMEM_LIMIT_BYTES = 56 * 1024 * 1024
SUBLANES = 8
LANES = 128


def _cparams(*sem):
    return pltpu.CompilerParams(dimension_semantics=sem, vmem_limit_bytes=VMEM_LIMIT_BYTES)


def _bdot(a, b):
    return jnp.dot(a.astype(BF16), b.astype(BF16), preferred_element_type=F32)


def _bdot_nt(a, b):
    return lax.dot_general(a.astype(BF16), b.astype(BF16), (((1,), (1,)), ((), ())), preferred_element_type=F32)


def _bdot_tn(a, b):
    return lax.dot_general(a.astype(BF16), b.astype(BF16), (((0,), (0,)), ((), ())), preferred_element_type=F32)


def _split2(x):
    hi = x.astype(BF16)
    lo = (x - hi.astype(F32)).astype(BF16)
    return hi, lo


def _split3(x):
    hi = x.astype(BF16)
    r = x - hi.astype(F32)
    mid = r.astype(BF16)
    lo = (r - mid.astype(F32)).astype(BF16)
    return hi, mid, lo


def _dot3(a, b):
    ah, al = _split2(a)
    bh, bl = _split2(b)
    d = functools.partial(jnp.dot, preferred_element_type=F32)
    return d(ah, bh) + (d(ah, bl) + d(al, bh))


def _cumsum_rows(x):
    L = x.shape[0]
    tri = (lax.broadcasted_iota(jnp.int32, (L, L), 0) >= lax.broadcasted_iota(jnp.int32, (L, L), 1)).astype(BF16)
    d = functools.partial(jnp.dot, preferred_element_type=F32)
    hi, mid, lo = _split3(x)
    return d(tri, hi) + (d(tri, mid) + d(tri, lo))


def _cumsum_cols(x):
    L = x.shape[1]
    tri = (lax.broadcasted_iota(jnp.int32, (L, L), 0) <= lax.broadcasted_iota(jnp.int32, (L, L), 1)).astype(BF16)
    d = functools.partial(jnp.dot, preferred_element_type=F32)
    hi, mid, lo = _split3(x)
    return d(hi, tri) + (d(mid, tri) + d(lo, tri))


def _lane_bcast_cols(xt):
    n, tb = xt.shape
    sel = ((lax.broadcasted_iota(jnp.int32, (tb, tb * LANES), 1) >> 7)
           == lax.broadcasted_iota(jnp.int32, (tb, tb * LANES), 0)).astype(BF16)
    d = functools.partial(jnp.dot, preferred_element_type=F32)
    hi, mid, lo = _split3(xt)
    return d(hi, sel) + (d(mid, sel) + d(lo, sel))


def _softplus(x):
    return jnp.maximum(x, 0.0) + jnp.log1p(jnp.exp(-jnp.abs(x)))


def _silu(x):
    return x * jax.nn.sigmoid(x)


def _rms(x):
    return x * lax.rsqrt(jnp.mean(x * x, axis=-1, keepdims=True) + NORM_EPS)


def _tri_masks(L):
    ri = lax.broadcasted_iota(jnp.int32, (L, L), 0)
    ci = lax.broadcasted_iota(jnp.int32, (L, L), 1)
    return ri, ci


def _rmsnorm_body(x_ref, w_ref, o_ref):
    o_ref[...] = (_rms(x_ref[...]) * w_ref[...]).astype(o_ref.dtype)


def rmsnorm(x, w, out_dtype):
    M, D = x.shape
    tm = min(M, 256)
    return pl.pallas_call(
        _rmsnorm_body,
        out_shape=jax.ShapeDtypeStruct((M, D), out_dtype),
        grid=(M // tm,),
        in_specs=[pl.BlockSpec((tm, D), lambda i: (i, 0)), pl.BlockSpec((1, D), lambda i: (0, 0))],
        out_specs=pl.BlockSpec((tm, D), lambda i: (i, 0)),
        compiler_params=_cparams("parallel"),
        name="rmsnorm",
    )(x, w.reshape(1, D))


def _mm_body(a_ref, b_ref, o_ref):
    o_ref[...] = jnp.dot(a_ref[...], b_ref[...], preferred_element_type=F32)


def _mm_res_body(a_ref, b_ref, r_ref, o_ref):
    o_ref[...] = r_ref[...] + jnp.dot(a_ref[...], b_ref[...], preferred_element_type=F32)


def _mm_tiles(M, K):
    if M <= 128:
        return M, (1024 if K <= 4096 else 512), False
    tm = min(M, 1024)
    if K <= 4096:
        return tm, 512, False
    if K <= 8192:
        return tm, 256, False
    return tm, 256, True


def matmul(a, b, res=None, name="matmul"):
    M, K = a.shape
    N = b.shape[1]
    tm, tn, single = _mm_tiles(M, K)
    tn = min(tn, N)
    a_spec = pl.BlockSpec((tm, K), lambda i, j: (i, 0), **({"pipeline_mode": pl.Buffered(1)} if single else {}))
    in_specs = [a_spec, pl.BlockSpec((K, tn), lambda i, j: (0, j))]
    args = [a, b]
    body = _mm_body
    if res is not None:
        in_specs.append(pl.BlockSpec((tm, tn), lambda i, j: (i, j)))
        args.append(res)
        body = _mm_res_body
    return pl.pallas_call(
        body,
        out_shape=jax.ShapeDtypeStruct((M, N), F32),
        grid=(M // tm, pl.cdiv(N, tn)),
        in_specs=in_specs,
        out_specs=pl.BlockSpec((tm, tn), lambda i, j: (i, j)),
        compiler_params=_cparams("parallel", "parallel"),
        name=name,
    )(*args)


FFN_TC = 256
FFN_TN = 2 * FFN_TC
D_FF_PAD = -(-D_FF // FFN_TN) * FFN_TN


def _ffn1_seq_body(x_ref, wg_ref, wu_ref, cw_ref, cb_ref, act_ref, tail_ref, carry_ref, gbuf_ref, *, tiles_per_seq):
    i = pl.program_id(0)
    j = pl.program_id(1)
    tm, tn = act_ref.shape
    tc = FFN_TC

    @pl.when((i == 0) & (j == 0))
    def _():
        carry_ref[...] = jnp.zeros(carry_ref.shape, F32)

    x = x_ref[...]
    for c in range(tn // tc):
        cs = slice(c * tc, (c + 1) * tc)
        g = jnp.dot(x, wg_ref[:, cs], preferred_element_type=F32)
        u = jnp.dot(x, wu_ref[:, cs], preferred_element_type=F32)
        gbuf_ref[c, 0:SUBLANES, :] = jnp.where(i % tiles_per_seq == 0, 0.0, carry_ref[j, :, cs])
        gbuf_ref[c, SUBLANES:SUBLANES + tm, :] = g
        w = cw_ref[:, cs]
        y = (gbuf_ref[c, SUBLANES - 2:SUBLANES - 2 + tm, :] * w[0:1]
             + gbuf_ref[c, SUBLANES - 1:SUBLANES - 1 + tm, :] * w[1:2] + g * w[2:3] + cb_ref[:, cs])
        act_ref[:, cs] = (_silu(y) * u).astype(act_ref.dtype)
        tail = g[tm - SUBLANES:tm, :]
        carry_ref[j, :, cs] = tail
        tail_ref[0, :, cs] = tail


def ffn1_seq(x, wg, wu, cw, cb, seq_len):
    M, D = x.shape
    F = wg.shape[1]
    tm = min(seq_len, 1024)
    tn = FFN_TN
    nj = F // tn
    return pl.pallas_call(
        functools.partial(_ffn1_seq_body, tiles_per_seq=seq_len // tm),
        out_shape=(jax.ShapeDtypeStruct((M, F), BF16), jax.ShapeDtypeStruct((M // tm, SUBLANES, F), F32)),
        grid=(M // tm, nj),
        in_specs=[
            pl.BlockSpec((tm, D), lambda i, j: (i, 0)),
            pl.BlockSpec((D, tn), lambda i, j: (0, j)),
            pl.BlockSpec((D, tn), lambda i, j: (0, j)),
            pl.BlockSpec((FFN_CONV, tn), lambda i, j: (0, j)),
            pl.BlockSpec((1, tn), lambda i, j: (0, j)),
        ],
        out_specs=(pl.BlockSpec((tm, tn), lambda i, j: (i, j)), pl.BlockSpec((1, SUBLANES, tn), lambda i, j: (i, 0, j))),
        scratch_shapes=[pltpu.VMEM((nj, SUBLANES, tn), F32), pltpu.VMEM((tn // FFN_TC, tm + SUBLANES, FFN_TC), F32)],
        compiler_params=_cparams("arbitrary", "arbitrary"),
        name="ffn1_seq",
    )(x, wg, wu, cw, cb.reshape(1, F))


def _ffn1_step_body(x_ref, wg_ref, wu_ref, cw_ref, cb_ref, s0_ref, s1_ref, act_ref, g_ref):
    x = x_ref[...]
    g = jnp.dot(x, wg_ref[...], preferred_element_type=F32)
    u = jnp.dot(x, wu_ref[...], preferred_element_type=F32)
    w = cw_ref[...]
    y = s0_ref[...] * w[0:1] + s1_ref[...] * w[1:2] + g * w[2:3] + cb_ref[...]
    act_ref[...] = (_silu(y) * u).astype(act_ref.dtype)
    g_ref[...] = g


def ffn1_step(x, wg, wu, cw, cb, s0, s1):
    M, D = x.shape
    F = wg.shape[1]
    tn = FFN_TN
    return pl.pallas_call(
        _ffn1_step_body,
        out_shape=(jax.ShapeDtypeStruct((M, F), BF16), jax.ShapeDtypeStruct((M, F), F32)),
        grid=(F // tn,),
        in_specs=[
            pl.BlockSpec((M, D), lambda j: (0, 0)),
            pl.BlockSpec((D, tn), lambda j: (0, j)),
            pl.BlockSpec((D, tn), lambda j: (0, j)),
            pl.BlockSpec((FFN_CONV, tn), lambda j: (0, j)),
            pl.BlockSpec((1, tn), lambda j: (0, j)),
            pl.BlockSpec((M, tn), lambda j: (0, j)),
            pl.BlockSpec((M, tn), lambda j: (0, j)),
        ],
        out_specs=(pl.BlockSpec((M, tn), lambda j: (0, j)), pl.BlockSpec((M, tn), lambda j: (0, j))),
        compiler_params=_cparams("parallel"),
        name="ffn1_step",
    )(x, wg, wu, cw, cb.reshape(1, F), s0, s1)


CONV_TC = 1024


def _conv_seq_body(x_ref, w_ref, b_ref, o_ref, buf_ref, *, width):
    t = pl.program_id(2)
    tt = x_ref.shape[0]

    @pl.when(t == 0)
    def _():
        buf_ref[0:SUBLANES, :] = jnp.zeros((SUBLANES, buf_ref.shape[1]), F32)

    x = x_ref[...]
    buf_ref[SUBLANES:SUBLANES + tt, :] = x
    w = w_ref[...]
    y = x * w[width - 1:width]
    for j in range(1, width):
        y = y + buf_ref[SUBLANES - j:SUBLANES - j + tt, :] * w[width - 1 - j:width - j]
    o_ref[...] = _silu(y + b_ref[...])
    buf_ref[0:SUBLANES, :] = x[tt - SUBLANES:tt, :]


def conv_seq(proj, col0, w, b, nseq, seq_len):
    width, C = w.shape
    tc = CONV_TC
    tt = min(seq_len, 512)
    nt = seq_len // tt
    c0 = col0 // tc
    return pl.pallas_call(
        functools.partial(_conv_seq_body, width=width),
        out_shape=jax.ShapeDtypeStruct((nseq * seq_len, C), F32),
        grid=(nseq, C // tc, nt),
        in_specs=[
            pl.BlockSpec((tt, tc), lambda s, c, t: (s * nt + t, c0 + c)),
            pl.BlockSpec((width, tc), lambda s, c, t: (0, c)),
            pl.BlockSpec((1, tc), lambda s, c, t: (0, c)),
        ],
        out_specs=pl.BlockSpec((tt, tc), lambda s, c, t: (s * nt + t, c)),
        scratch_shapes=[pltpu.VMEM((tt + SUBLANES, tc), F32)],
        compiler_params=_cparams("parallel", "parallel", "arbitrary"),
        name="conv_seq",
    )(proj, w, b.reshape(1, C))


def _conv_step_body(x_ref, s0_ref, s1_ref, s2_ref, w_ref, b_ref, o_ref):
    w = w_ref[...]
    y = s0_ref[...] * w[0:1] + s1_ref[...] * w[1:2] + s2_ref[...] * w[2:3] + x_ref[...] * w[3:4] + b_ref[...]
    o_ref[...] = _silu(y)


def conv_step(proj, col0, w, b, hist):
    width, C = w.shape
    M = proj.shape[0]
    tc = CONV_TC
    c0 = col0 // tc
    spec = pl.BlockSpec((M, tc), lambda c: (0, c))
    return pl.pallas_call(
        _conv_step_body,
        out_shape=jax.ShapeDtypeStruct((M, C), F32),
        grid=(C // tc,),
        in_specs=[pl.BlockSpec((M, tc), lambda c: (0, c0 + c)), spec, spec, spec,
                  pl.BlockSpec((width, tc), lambda c: (0, c)), pl.BlockSpec((1, tc), lambda c: (0, c))],
        out_specs=spec,
        compiler_params=_cparams("parallel"),
        name="conv_step",
    )(proj, hist[:, 0], hist[:, 1], hist[:, 2], w, b.reshape(1, C))


def _ret_log_gamma(h):
    return jnp.log(1.0 - jnp.exp2(-5.0 - jnp.full((1, 1), h, jnp.int32).astype(F32)))


def _rope(x, cos, sin):
    half = x.shape[-1] // 2
    x1, x2 = x[:, :half], x[:, half:]
    return jnp.concatenate([x1 * cos - x2 * sin, x1 * sin + x2 * cos], axis=-1)


def _ret_chunk_body(q_ref, k_ref, v_ref, g_ref, cos_ref, sin_ref, o_ref, s_ref):
    h = pl.program_id(1)
    c = pl.program_id(2)
    L = q_ref.shape[0]

    @pl.when(c == 0)
    def _():
        s_ref[...] = jnp.zeros(s_ref.shape, F32)

    lg = _ret_log_gamma(h)
    ri, ci = _tri_masks(L)
    rel = (ri - ci).astype(F32)
    decay_in = jnp.where(rel >= 0, jnp.exp(lg * jnp.maximum(rel, 0.0)), 0.0)
    idx = lax.broadcasted_iota(jnp.int32, (L, 1), 0).astype(F32)
    decay_q = jnp.exp((idx + 1.0) * lg)
    decay_k = jnp.exp((L - 1.0 - idx) * lg)
    decay_s = jnp.exp(L * lg)
    cos, sin = cos_ref[...], sin_ref[...]
    q = _rope(q_ref[...], cos, sin)
    k = _rope(k_ref[...], cos, sin) * (RET_DK ** -0.5)
    v = v_ref[...]
    s = s_ref[0, 0]
    sc = _bdot_nt(q, k) * decay_in
    o = _bdot(sc, v) + _bdot(q, s) * decay_q
    s_ref[0, 0] = decay_s * s + _bdot_tn(k * decay_k, v)
    o_ref[...] = (_silu(g_ref[...]) * _rms(o)).astype(o_ref.dtype)


def _rope_tables(pos):
    half = RET_DK // 2
    inv_freq = 1.0 / (ROPE_BASE ** jnp.linspace(0.0, 1.0, half, dtype=F32))
    ang = pos.astype(F32)[:, None] * inv_freq[None, :]
    return jnp.cos(ang), jnp.sin(ang)


def ret_prompt(proj, nseq, seq_len):
    L = min(CHUNK, seq_len)
    nc = seq_len // L
    M = nseq * seq_len
    cos, sin = _rope_tables(jnp.arange(seq_len))
    kq, kv, kg = RET_QK_DIM // RET_DK, (2 * RET_QK_DIM) // RET_DV, (2 * RET_QK_DIM + RET_V_DIM) // RET_DV
    return pl.pallas_call(
        _ret_chunk_body,
        out_shape=(jax.ShapeDtypeStruct((M, RET_V_DIM), BF16),
                   jax.ShapeDtypeStruct((nseq, RET_HEADS, RET_DK, RET_DV), F32)),
        grid=(nseq, RET_HEADS, nc),
        in_specs=[
            pl.BlockSpec((L, RET_DK), lambda b, h, c: (b * nc + c, h)),
            pl.BlockSpec((L, RET_DK), lambda b, h, c: (b * nc + c, kq + h)),
            pl.BlockSpec((L, RET_DV), lambda b, h, c: (b * nc + c, kv + h)),
            pl.BlockSpec((L, RET_DV), lambda b, h, c: (b * nc + c, kg + h)),
            pl.BlockSpec((L, RET_DK // 2), lambda b, h, c: (c, 0)),
            pl.BlockSpec((L, RET_DK // 2), lambda b, h, c: (c, 0)),
        ],
        out_specs=(pl.BlockSpec((L, RET_DV), lambda b, h, c: (b * nc + c, h)),
                   pl.BlockSpec((1, 1, RET_DK, RET_DV), lambda b, h, c: (b, h, 0, 0))),
        compiler_params=_cparams("parallel", "parallel", "arbitrary"),
        name="ret_chunk",
    )(proj, proj, proj, proj, cos, sin)


STEP_TB = 8


def _ret_step_body(q_ref, k_ref, v_ref, g_ref, cos_ref, sin_ref, s_ref, o_ref, so_ref):
    h = pl.program_id(1)
    tb = q_ref.shape[0]
    gamma = jnp.exp(_ret_log_gamma(h))
    cos, sin = cos_ref[...], sin_ref[...]
    q = _rope(q_ref[...], cos, sin)
    k = _rope(k_ref[...], cos, sin) * (RET_DK ** -0.5)
    v = v_ref[...]
    g = g_ref[...]
    sc = jnp.sum(q * k, axis=-1, keepdims=True)
    qt = q.T
    kt = k.T
    for b in range(tb):
        s = s_ref[b, 0]
        vb = v[b:b + 1, :]
        qs = jnp.sum(s * qt[:, b:b + 1], axis=0, keepdims=True)
        o = sc[b:b + 1, :] * vb + qs * gamma
        so_ref[b, 0] = gamma * s + kt[:, b:b + 1] * vb
        o_ref[b:b + 1, :] = (_silu(g[b:b + 1, :]) * _rms(o)).astype(o_ref.dtype)


def ret_step(proj, state, pos):
    M = proj.shape[0]
    tb = STEP_TB
    cos, sin = _rope_tables(jnp.full((1,), pos))
    kq, kv, kg = RET_QK_DIM // RET_DK, (2 * RET_QK_DIM) // RET_DV, (2 * RET_QK_DIM + RET_V_DIM) // RET_DV
    s_spec = pl.BlockSpec((tb, 1, RET_DK, RET_DV), lambda i, h: (i, h, 0, 0))
    return pl.pallas_call(
        _ret_step_body,
        out_shape=(jax.ShapeDtypeStruct((M, RET_V_DIM), BF16), jax.ShapeDtypeStruct(state.shape, F32)),
        grid=(M // tb, RET_HEADS),
        in_specs=[
            pl.BlockSpec((tb, RET_DK), lambda i, h: (i, h)),
            pl.BlockSpec((tb, RET_DK), lambda i, h: (i, kq + h)),
            pl.BlockSpec((tb, RET_DV), lambda i, h: (i, kv + h)),
            pl.BlockSpec((tb, RET_DV), lambda i, h: (i, kg + h)),
            pl.BlockSpec((1, RET_DK // 2), lambda i, h: (0, 0)),
            pl.BlockSpec((1, RET_DK // 2), lambda i, h: (0, 0)),
            s_spec,
        ],
        out_specs=(pl.BlockSpec((tb, RET_DV), lambda i, h: (i, h)), s_spec),
        compiler_params=_cparams("parallel", "parallel"),
        name="ret_step",
    )(proj, proj, proj, proj, cos, sin, state)


def _ssm_chunk_body(x_ref, b_ref, c_ref, z_ref, dtr_ref, dtc_ref, pr_ref, pc_ref, dskip_ref, nw_ref, y_ref, s_ref):
    c = pl.program_id(2)
    L = x_ref.shape[0]
    P2 = 2 * SSM_HEAD_DIM

    @pl.when(c == 0)
    def _():
        s_ref[...] = jnp.zeros(s_ref.shape, F32)

    pr = pr_ref[0]
    pc = pc_ref[0]
    dt_r = _softplus(dtr_ref[0] + pr[0:1, :])
    dt_c = _softplus(dtc_ref[0] + pc[:, 0:1])
    acs_r = _cumsum_rows(dt_r * (-jnp.exp(pr[1:2, :])))
    acs_c = _cumsum_cols(dt_c * (-jnp.exp(pc[:, 1:2])))
    bm = b_ref[...]
    cm = c_ref[...]
    x = x_ref[...]
    cb = _bdot_nt(cm, bm)
    ri, ci = _tri_masks(L)
    incl = ri >= ci
    lo_lane = lax.broadcasted_iota(jnp.int32, (L, P2), 1) < SSM_HEAD_DIM
    lo_row = lax.broadcasted_iota(jnp.int32, (P2, 1), 0) < SSM_HEAD_DIM
    ys = []
    for p in range(SSM_HPG // 2):
        h0, h1 = 2 * p, 2 * p + 1
        sc = []
        for h in (h0, h1):
            seg = acs_r[:, h:h + 1] - acs_c[h:h + 1, :]
            sc.append(cb * jnp.exp(jnp.where(incl, seg, -jnp.inf)) * dt_c[h:h + 1, :])
        xp = x[:, p * P2:(p + 1) * P2]
        x2 = jnp.concatenate([jnp.where(lo_lane, xp, 0.0), jnp.where(lo_lane, 0.0, xp)], axis=0)
        y = _bdot(jnp.concatenate(sc, axis=1), x2)
        st = s_ref[0, h0:h0 + 2].reshape(P2, SSM_STATE)
        ea = jnp.where(lo_lane, jnp.exp(acs_r[:, h0:h0 + 1]), jnp.exp(acs_r[:, h1:h1 + 1]))
        y = y + _bdot_nt(cm, st) * ea
        a_end0 = acs_r[L - 1:L, h0:h0 + 1]
        a_end1 = acs_r[L - 1:L, h1:h1 + 1]
        w_end = jnp.where(lo_lane, jnp.exp(a_end0 - acs_r[:, h0:h0 + 1]) * dt_r[:, h0:h0 + 1],
                          jnp.exp(a_end1 - acs_r[:, h1:h1 + 1]) * dt_r[:, h1:h1 + 1])
        st_new = jnp.where(lo_row, jnp.exp(a_end0), jnp.exp(a_end1)) * st + _bdot_tn(xp * w_end, bm)
        s_ref[0, h0:h0 + 2] = st_new.reshape(2, SSM_HEAD_DIM, SSM_STATE)
        ys.append(y)
    y = jnp.concatenate(ys, axis=1) + dskip_ref[...] * x
    y = y * _silu(z_ref[...])
    y_ref[...] = (_rms(y) * nw_ref[...]).astype(y_ref.dtype)


def _ssm_params(dt_bias, a_log):
    pr = jnp.stack([dt_bias.reshape(SSM_GROUPS, SSM_HPG), a_log.reshape(SSM_GROUPS, SSM_HPG)], axis=1)
    return pr, jnp.swapaxes(pr, 1, 2)


def ssm_prompt(proj, xa, dt_bias, a_log, d_skip, norm_w, nseq, seq_len):
    L = min(CHUNK, seq_len)
    nc = seq_len // L
    M = nseq * seq_len
    W = SSM_GROUP_W
    dt_raw = proj[:, SSM_D_INNER + SSM_CONV_DIM:]
    dtr = jnp.transpose(dt_raw.reshape(M, SSM_GROUPS, SSM_HPG), (1, 0, 2))
    dtc = jnp.swapaxes(dtr, 1, 2)
    pr, pc = _ssm_params(dt_bias, a_log)
    dskip = jnp.repeat(d_skip, SSM_HEAD_DIM).reshape(1, SSM_D_INNER)
    kb = SSM_D_INNER // SSM_STATE
    return pl.pallas_call(
        _ssm_chunk_body,
        out_shape=(jax.ShapeDtypeStruct((M, SSM_D_INNER), BF16),
                   jax.ShapeDtypeStruct((nseq, SSM_HEADS, SSM_HEAD_DIM, SSM_STATE), F32)),
        grid=(nseq, SSM_GROUPS, nc),
        in_specs=[
            pl.BlockSpec((L, W), lambda b, g, c: (b * nc + c, g)),
            pl.BlockSpec((L, SSM_STATE), lambda b, g, c: (b * nc + c, kb + g)),
            pl.BlockSpec((L, SSM_STATE), lambda b, g, c: (b * nc + c, kb + SSM_GROUPS + g)),
            pl.BlockSpec((L, W), lambda b, g, c: (b * nc + c, g)),
            pl.BlockSpec((1, L, SSM_HPG), lambda b, g, c: (g, b * nc + c, 0)),
            pl.BlockSpec((1, SSM_HPG, L), lambda b, g, c: (g, 0, b * nc + c)),
            pl.BlockSpec((1, 2, SSM_HPG), lambda b, g, c: (g, 0, 0)),
            pl.BlockSpec((1, SSM_HPG, 2), lambda b, g, c: (g, 0, 0)),
            pl.BlockSpec((1, W), lambda b, g, c: (0, g)),
            pl.BlockSpec((1, W), lambda b, g, c: (0, g)),
        ],
        out_specs=(pl.BlockSpec((L, W), lambda b, g, c: (b * nc + c, g)),
                   pl.BlockSpec((1, SSM_HPG, SSM_HEAD_DIM, SSM_STATE), lambda b, g, c: (b, g, 0, 0))),
        compiler_params=_cparams("parallel", "parallel", "arbitrary"),
        name="ssm_chunk",
    )(xa, xa, xa, proj, dtr, dtc, pr, pc, dskip, norm_w.reshape(1, SSM_D_INNER))


def _ssm_step_body(x_ref, b_ref, c_ref, z_ref, dtx_ref, bias_ref, alog_ref, dskip_ref, nw_ref, s_ref, y_ref, so_ref):
    tb = x_ref.shape[0]
    W = SSM_GROUP_W
    x = x_ref[...]
    bm = b_ref[...]
    cm = c_ref[...]
    dt = _softplus(dtx_ref[...] + bias_ref[...])
    da = jnp.exp(dt * (-jnp.exp(alog_ref[...])))
    cb = jnp.sum(cm * bm, axis=-1, keepdims=True)
    xdt = (x * dt).T
    dat = da.T
    rows = lax.broadcasted_iota(jnp.int32, (tb, W), 0)
    cs = jnp.zeros((tb, W), F32)
    for b in range(tb):
        st = s_ref[b].reshape(W, SSM_STATE)
        cs = jnp.where(rows == b, _bdot_nt(cm, st), cs)
        st_new = dat[:, b:b + 1] * st + xdt[:, b:b + 1] * bm[b:b + 1, :]
        so_ref[b] = st_new.reshape(SSM_HPG, SSM_HEAD_DIM, SSM_STATE)
    y = cb * dt * x + cs * da + dskip_ref[...] * x
    y = y * _silu(z_ref[...])
    y_ref[...] = (_rms(y) * nw_ref[...]).astype(y_ref.dtype)


def ssm_step(proj, xa, state, dt_bias, a_log, d_skip, norm_w):
    M = proj.shape[0]
    tb = STEP_TB
    W = SSM_GROUP_W
    rep = lambda a: jnp.repeat(a, SSM_HEAD_DIM, axis=-1)
    dtx = rep(proj[:, SSM_D_INNER + SSM_CONV_DIM:])
    kb = SSM_D_INNER // SSM_STATE
    s_spec = pl.BlockSpec((tb, SSM_HPG, SSM_HEAD_DIM, SSM_STATE), lambda g, i: (i, g, 0, 0))
    row = pl.BlockSpec((1, W), lambda g, i: (0, g))
    tile = pl.BlockSpec((tb, W), lambda g, i: (i, g))
    return pl.pallas_call(
        _ssm_step_body,
        out_shape=(jax.ShapeDtypeStruct((M, SSM_D_INNER), BF16), jax.ShapeDtypeStruct(state.shape, F32)),
        grid=(SSM_GROUPS, M // tb),
        in_specs=[
            tile,
            pl.BlockSpec((tb, SSM_STATE), lambda g, i: (i, kb + g)),
            pl.BlockSpec((tb, SSM_STATE), lambda g, i: (i, kb + SSM_GROUPS + g)),
            tile, tile, row, row, row, row, s_spec,
        ],
        out_specs=(tile, s_spec),
        compiler_params=_cparams("parallel", "parallel"),
        name="ssm_step",
    )(xa, xa, xa, proj, dtx, rep(dt_bias).reshape(1, -1), rep(a_log).reshape(1, -1), rep(d_skip).reshape(1, -1),
      norm_w.reshape(1, -1), state)


def _l2n(x):
    return x * lax.rsqrt(jnp.sum(x * x, axis=-1, keepdims=True) + NORM_EPS)


def _unit_lower_inverses(mats, ri, ci):
    L = mats[0].shape[0]
    eye = (ri == ci).astype(F32)
    same8 = (ri >> 3) == (ci >> 3)
    ns = [jnp.where(same8, -a, 0.0) for a in mats]
    n2 = [_bdot(n, n) for n in ns]
    n4 = [_bdot(x, x) for x in n2]
    ts = [eye + n for n in ns]
    ts = [t + _bdot(t, x) for t, x in zip(ts, n2)]
    ts = [t + _bdot(t, x) for t, x in zip(ts, n4)]
    sh = 3
    while (1 << sh) < L:
        lower_left = ((ri >> (sh + 1)) == (ci >> (sh + 1))) & ((ri >> sh) != (ci >> sh))
        us = [_bdot(jnp.where(lower_left, a, 0.0), t) for a, t in zip(mats, ts)]
        ts = [t - _bdot(t, u) for t, u in zip(ts, us)]
        sh += 1
    return ts


def _gdn_chunk_body(q_ref, k_ref, v_ref, z_ref, gr_ref, gc_ref, pr_ref, pc_ref, nw_ref, o_ref, s_ref, *, ks):
    c = pl.program_id(2)
    L = q_ref.shape[0]
    vs = ks * GDN_REP

    @pl.when(c == 0)
    def _():
        s_ref[...] = jnp.zeros(s_ref.shape, F32)

    gr = gr_ref[0]
    gc = gc_ref[0]
    pr = pr_ref[0]
    pc = pc_ref[0]
    beta = jax.nn.sigmoid(gr[:, 0:vs])
    gt_r = _cumsum_rows(-jnp.exp(pr[0:1, :]) * _softplus(gr[:, vs:] + pr[1:2, :]))
    gt_c = _cumsum_cols(-jnp.exp(pc[:, 0:1]) * _softplus(gc[vs:, :] + pc[:, 1:2]))
    ri, ci = _tri_masks(L)
    nw = nw_ref[...]
    qs, ks_ = [], []
    for kh in range(ks):
        qs.append(_l2n(q_ref[:, kh * GDN_DK:(kh + 1) * GDN_DK]) * (GDN_DK ** -0.5))
        ks_.append(_l2n(k_ref[:, kh * GDN_DK:(kh + 1) * GDN_DK]))
    kks = [_bdot_nt(k, k) for k in ks_]
    qks = [_bdot_nt(q, k) for q, k in zip(qs, ks_)]
    heads = range(vs)
    kof = [j // GDN_REP for j in heads]
    st = [s_ref[0, j] for j in heads]
    gcol = [gt_r[:, j:j + 1] for j in heads]
    seg = [gcol[j] - gt_c[j:j + 1, :] for j in heads]
    bcol = [beta[:, j:j + 1] for j in heads]
    eg = [jnp.exp(gcol[j]) for j in heads]
    a_mats = [kks[kof[j]] * jnp.exp(jnp.where(ri > ci, seg[j], -jnp.inf)) * bcol[j] for j in heads]
    rhs = [bcol[j] * (v_ref[:, j * GDN_DV:(j + 1) * GDN_DV] - eg[j] * _bdot(ks_[kof[j]], st[j])) for j in heads]
    tinv = _unit_lower_inverses(a_mats, ri, ci)
    delta = [_bdot(tinv[j], rhs[j]) for j in heads]
    o = [eg[j] * _bdot(qs[kof[j]], st[j])
         + _bdot(qks[kof[j]] * jnp.exp(jnp.where(ri >= ci, seg[j], -jnp.inf)), delta[j]) for j in heads]
    g_end = [gt_r[L - 1:L, j:j + 1] for j in heads]
    s_new = [jnp.exp(g_end[j]) * st[j] + _bdot_tn(ks_[kof[j]] * jnp.exp(g_end[j] - gcol[j]), delta[j]) for j in heads]
    for j in heads:
        s_ref[0, j] = s_new[j]
        ze = z_ref[:, j * GDN_DV:(j + 1) * GDN_DV]
        o_ref[:, j * GDN_DV:(j + 1) * GDN_DV] = (_rms(o[j]) * nw * _silu(ze)).astype(o_ref.dtype)


def _gdn_gate_inputs(proj, ks):
    M = proj.shape[0]
    ng, vs = GDN_K_HEADS // ks, ks * GDN_REP
    tail = proj[:, GDN_CONV_DIM + GDN_VAL_DIM:]
    braw = tail[:, :GDN_V_HEADS].reshape(M, ng, vs)
    araw = tail[:, GDN_V_HEADS:].reshape(M, ng, vs)
    gr = jnp.transpose(jnp.concatenate([braw, araw], axis=-1), (1, 0, 2))
    return gr, jnp.swapaxes(gr, 1, 2)


def _gdn_params(a_log, dt_bias, ks):
    ng, vs = GDN_K_HEADS // ks, ks * GDN_REP
    pr = jnp.stack([a_log.reshape(ng, vs), dt_bias.reshape(ng, vs)], axis=1)
    return pr, jnp.swapaxes(pr, 1, 2)


GDN_KH_STEP = 4


def gdn_prompt(proj, qkv, a_log, dt_bias, norm_w, nseq, seq_len):
    L = min(CHUNK, seq_len)
    nc = seq_len // L
    M = nseq * seq_len
    ks = GDN_KH_STEP
    ng, vs = GDN_K_HEADS // ks, ks * GDN_REP
    KW, VW = ks * GDN_DK, vs * GDN_DV
    gr, gc = _gdn_gate_inputs(proj, ks)
    pr, pc = _gdn_params(a_log, dt_bias, ks)
    kk, kv, kz = GDN_KEY_DIM // KW, (2 * GDN_KEY_DIM) // VW, GDN_CONV_DIM // VW
    return pl.pallas_call(
        functools.partial(_gdn_chunk_body, ks=ks),
        out_shape=(jax.ShapeDtypeStruct((M, GDN_VAL_DIM), BF16),
                   jax.ShapeDtypeStruct((nseq, GDN_V_HEADS, GDN_DK, GDN_DV), F32)),
        grid=(nseq, ng, nc),
        in_specs=[
            pl.BlockSpec((L, KW), lambda b, h, c: (b * nc + c, h)),
            pl.BlockSpec((L, KW), lambda b, h, c: (b * nc + c, kk + h)),
            pl.BlockSpec((L, VW), lambda b, h, c: (b * nc + c, kv + h)),
            pl.BlockSpec((L, VW), lambda b, h, c: (b * nc + c, kz + h)),
            pl.BlockSpec((1, L, 2 * vs), lambda b, h, c: (h, b * nc + c, 0)),
            pl.BlockSpec((1, 2 * vs, L), lambda b, h, c: (h, 0, b * nc + c)),
            pl.BlockSpec((1, 2, vs), lambda b, h, c: (h, 0, 0)),
            pl.BlockSpec((1, vs, 2), lambda b, h, c: (h, 0, 0)),
            pl.BlockSpec((1, GDN_DV), lambda b, h, c: (0, 0)),
        ],
        out_specs=(pl.BlockSpec((L, VW), lambda b, h, c: (b * nc + c, h)),
                   pl.BlockSpec((1, vs, GDN_DK, GDN_DV), lambda b, h, c: (b, h, 0, 0))),
        compiler_params=_cparams("parallel", "parallel", "arbitrary"),
        name="gdn_chunk",
    )(qkv, qkv, qkv, proj, gr, gc, pr, pc, norm_w.reshape(1, GDN_DV))


def _gdn_step_body(q_ref, k_ref, v_ref, z_ref, gr_ref, pr_ref, nw_ref, s_ref, o_ref, so_ref, *, ks):
    tb = q_ref.shape[0]
    vs = ks * GDN_REP
    gr = gr_ref[0]
    pr = pr_ref[0]
    beta = jax.nn.sigmoid(gr[:, 0:vs])
    eg_all = jnp.exp(-jnp.exp(pr[0:1, :]) * _softplus(gr[:, vs:] + pr[1:2, :]))
    nw = nw_ref[...]
    for kh in range(ks):
        q = _l2n(q_ref[:, kh * GDN_DK:(kh + 1) * GDN_DK]) * (GDN_DK ** -0.5)
        k = _l2n(k_ref[:, kh * GDN_DK:(kh + 1) * GDN_DK])
        qk = jnp.sum(q * k, axis=-1, keepdims=True)
        qb = _lane_bcast_cols(q.T)
        kb = _lane_bcast_cols(k.T)
        for b in range(tb):
            kcol = kb[:, b * LANES:(b + 1) * LANES]
            qcol = qb[:, b * LANES:(b + 1) * LANES]
            for e in range(GDN_REP):
                j = kh * GDN_REP + e
                s = s_ref[b, j]
                eg = eg_all[b:b + 1, j:j + 1]
                k_s = jnp.sum(s * kcol, axis=0, keepdims=True)
                q_s = jnp.sum(s * qcol, axis=0, keepdims=True)
                delta = beta[b:b + 1, j:j + 1] * (v_ref[b:b + 1, j * GDN_DV:(j + 1) * GDN_DV] - eg * k_s)
                o = eg * q_s + qk[b:b + 1, :] * delta
                so_ref[b, j] = eg * s + kcol * delta
                ze = z_ref[b:b + 1, j * GDN_DV:(j + 1) * GDN_DV]
                o_ref[b:b + 1, j * GDN_DV:(j + 1) * GDN_DV] = (_rms(o) * nw * _silu(ze)).astype(o_ref.dtype)


GDN_KH_STEP_DECODE = 4


def gdn_step(proj, qkv, state, a_log, dt_bias, norm_w):
    M = proj.shape[0]
    tb = STEP_TB
    ks = GDN_KH_STEP_DECODE
    ng, vs = GDN_K_HEADS // ks, ks * GDN_REP
    KW, VW = ks * GDN_DK, vs * GDN_DV
    gr, _ = _gdn_gate_inputs(proj, ks)
    pr, _ = _gdn_params(a_log, dt_bias, ks)
    kk, kv, kz = GDN_KEY_DIM // KW, (2 * GDN_KEY_DIM) // VW, GDN_CONV_DIM // VW
    s_spec = pl.BlockSpec((tb, vs, GDN_DK, GDN_DV), lambda i, h: (i, h, 0, 0))
    return pl.pallas_call(
        functools.partial(_gdn_step_body, ks=ks),
        out_shape=(jax.ShapeDtypeStruct((M, GDN_VAL_DIM), BF16), jax.ShapeDtypeStruct(state.shape, F32)),
        grid=(M // tb, ng),
        in_specs=[
            pl.BlockSpec((tb, KW), lambda i, h: (i, h)),
            pl.BlockSpec((tb, KW), lambda i, h: (i, kk + h)),
            pl.BlockSpec((tb, VW), lambda i, h: (i, kv + h)),
            pl.BlockSpec((tb, VW), lambda i, h: (i, kz + h)),
            pl.BlockSpec((1, tb, 2 * vs), lambda i, h: (h, i, 0)),
            pl.BlockSpec((1, 2, vs), lambda i, h: (h, 0, 0)),
            pl.BlockSpec((1, GDN_DV), lambda i, h: (0, 0)),
            s_spec,
        ],
        out_specs=(pl.BlockSpec((tb, VW), lambda i, h: (i, h)), s_spec),
        compiler_params=_cparams("parallel", "parallel"),
        name="gdn_step",
    )(qkv, qkv, qkv, proj, gr, pr, norm_w.reshape(1, GDN_DV), state)


def _log_sigmoid(x):
    return -_softplus(-x)


def _ml_chunk_body(q_ref, k_ref, v_ref, og_ref, gr_ref, gc_ref, pr_ref, pc_ref, nw_ref, o_ref, c_ref, n_ref, m_ref):
    ch = pl.program_id(2)
    L = q_ref.shape[0]

    @pl.when(ch == 0)
    def _():
        c_ref[...] = jnp.zeros(c_ref.shape, F32)
        n_ref[...] = jnp.zeros(n_ref.shape, F32)
        m_ref[...] = jnp.zeros(m_ref.shape, F32)

    q = q_ref[...]
    k = k_ref[...] * (ML_DQK ** -0.5)
    v = v_ref[...]
    gr = gr_ref[0] + pr_ref[0]
    gc = gc_ref[0] + pc_ref[0]
    it_c = gr[:, 0:1]
    it_r = gc[0:1, :]
    ft_c = _cumsum_rows(_log_sigmoid(gr[:, 1:2]))
    ft_r = _cumsum_cols(_log_sigmoid(gc[1:2, :]))
    c_st = c_ref[0, 0]
    n_st = n_ref[0, 0]
    m_st = m_ref[0, 0][:, 0:1]
    ri, ci = _tri_masks(L)
    dmat = jnp.where(ri >= ci, ft_c - ft_r + it_r, -jnp.inf)
    inter = ft_c + m_st
    mi = jnp.maximum(inter, jnp.max(dmat, axis=-1, keepdims=True))
    wmat = jnp.exp(dmat - mi)
    winter = jnp.exp(inter - mi)
    sc = _bdot_nt(q, k) * wmat
    num = winter * _bdot(q, c_st) + _bdot(sc, v)
    den = winter * jnp.sum(q * n_st, axis=-1, keepdims=True) + jnp.sum(sc, axis=-1, keepdims=True)
    hh = num / jnp.maximum(jnp.abs(den), jnp.exp(-mi))
    m_new = mi[L - 1:L, :]
    f_end = ft_c[L - 1:L, :]
    wk = jnp.exp(f_end - ft_c + it_c - m_new)
    keep = jnp.exp(f_end + m_st - m_new)
    kw = k * wk
    c_ref[0, 0] = keep * c_st + _bdot_tn(kw, v)
    n_ref[0, 0] = keep * n_st + jnp.sum(kw, axis=0, keepdims=True)
    m_ref[0, 0] = jnp.broadcast_to(m_new, (1, LANES))
    o_ref[...] = (_rms(hh) * nw_ref[...] * jax.nn.sigmoid(og_ref[...])).astype(o_ref.dtype)


def _ml_gate_inputs(proj, gate_b):
    M = proj.shape[0]
    tail = proj[:, 2 * ML_QK_DIM + 2 * ML_V_DIM:]
    gr = jnp.transpose(tail.reshape(M, 2, ML_HEADS), (2, 0, 1))
    pr = jnp.transpose(gate_b.reshape(2, ML_HEADS), (1, 0)).reshape(ML_HEADS, 1, 2)
    return gr, jnp.swapaxes(gr, 1, 2), pr, jnp.swapaxes(pr, 1, 2)


def ml_prompt(proj, gate_b, norm_w, nseq, seq_len):
    L = min(CHUNK, seq_len)
    nc = seq_len // L
    M = nseq * seq_len
    gr, gc, pr, pc = _ml_gate_inputs(proj, gate_b)
    kk, kv, ko = ML_QK_DIM // ML_DQK, (2 * ML_QK_DIM) // ML_DV, (2 * ML_QK_DIM + ML_V_DIM) // ML_DV
    return pl.pallas_call(
        _ml_chunk_body,
        out_shape=(jax.ShapeDtypeStruct((M, ML_V_DIM), BF16),
                   jax.ShapeDtypeStruct((nseq, ML_HEADS, ML_DQK, ML_DV), F32),
                   jax.ShapeDtypeStruct((nseq, ML_HEADS, 1, ML_DQK), F32),
                   jax.ShapeDtypeStruct((nseq, ML_HEADS, 1, LANES), F32)),
        grid=(nseq, ML_HEADS, nc),
        in_specs=[
            pl.BlockSpec((L, ML_DQK), lambda b, h, c: (b * nc + c, h)),
            pl.BlockSpec((L, ML_DQK), lambda b, h, c: (b * nc + c, kk + h)),
            pl.BlockSpec((L, ML_DV), lambda b, h, c: (b * nc + c, kv + h)),
            pl.BlockSpec((L, ML_DV), lambda b, h, c: (b * nc + c, ko + h)),
            pl.BlockSpec((1, L, 2), lambda b, h, c: (h, b * nc + c, 0)),
            pl.BlockSpec((1, 2, L), lambda b, h, c: (h, 0, b * nc + c)),
            pl.BlockSpec((1, 1, 2), lambda b, h, c: (h, 0, 0)),
            pl.BlockSpec((1, 2, 1), lambda b, h, c: (h, 0, 0)),
            pl.BlockSpec((1, ML_DV), lambda b, h, c: (0, h)),
        ],
        out_specs=(pl.BlockSpec((L, ML_DV), lambda b, h, c: (b * nc + c, h)),
                   pl.BlockSpec((1, 1, ML_DQK, ML_DV), lambda b, h, c: (b, h, 0, 0)),
                   pl.BlockSpec((1, 1, 1, ML_DQK), lambda b, h, c: (b, h, 0, 0)),
                   pl.BlockSpec((1, 1, 1, LANES), lambda b, h, c: (b, h, 0, 0))),
        compiler_params=_cparams("parallel", "parallel", "arbitrary"),
        name="ml_chunk",
    )(proj, proj, proj, proj, gr, gc, pr, pc, norm_w.reshape(1, ML_V_DIM))


def _ml_step_body(q_ref, k_ref, v_ref, og_ref, gr_ref, pr_ref, nw_ref, c_ref, n_ref, m_ref, o_ref, co_ref, no_ref, mo_ref):
    tb = q_ref.shape[0]
    q = q_ref[...]
    k = k_ref[...] * (ML_DQK ** -0.5)
    v = v_ref[...]
    og = og_ref[...]
    gr = gr_ref[0] + pr_ref[0]
    it = gr[:, 0:1]
    ft = _log_sigmoid(gr[:, 1:2])
    qk = jnp.sum(q * k, axis=-1, keepdims=True)
    qt = _lane_bcast_cols(q.T)
    kt = _lane_bcast_cols(k.T)
    nw = nw_ref[...]
    reps = ML_DV // LANES
    for b in range(tb):
        qcol = jnp.concatenate([qt[:, b * LANES:(b + 1) * LANES]] * reps, axis=1)
        kcol = jnp.concatenate([kt[:, b * LANES:(b + 1) * LANES]] * reps, axis=1)
        c_st = c_ref[b, 0]
        n_st = n_ref[b, 0]
        m_st = m_ref[b, 0][:, 0:1]
        qb = q[b:b + 1, :]
        kb = k[b:b + 1, :]
        vb = v[b:b + 1, :]
        itb = it[b:b + 1, :]
        ftb = ft[b:b + 1, :]
        inter = ftb + m_st
        mi = jnp.maximum(inter, itb)
        wmat = jnp.exp(itb - mi)
        winter = jnp.exp(inter - mi)
        sc = qk[b:b + 1, :] * wmat
        num = winter * jnp.sum(c_st * qcol, axis=0, keepdims=True) + sc * vb
        den = winter * jnp.sum(qb * n_st, axis=-1, keepdims=True) + sc
        hh = num / jnp.maximum(jnp.abs(den), jnp.exp(-mi))
        wk = jnp.exp(itb - mi)
        keep = jnp.exp(ftb + m_st - mi)
        co_ref[b, 0] = keep * c_st + (kcol * wk) * vb
        no_ref[b, 0] = keep * n_st + kb * wk
        mo_ref[b, 0] = jnp.broadcast_to(mi, (1, LANES))
        o_ref[b:b + 1, :] = (_rms(hh) * nw * jax.nn.sigmoid(og[b:b + 1, :])).astype(o_ref.dtype)


def ml_step(proj, c0, n0, m0, gate_b, norm_w):
    M = proj.shape[0]
    tb = STEP_TB
    gr, _, pr, _ = _ml_gate_inputs(proj, gate_b)
    kk, kv, ko = ML_QK_DIM // ML_DQK, (2 * ML_QK_DIM) // ML_DV, (2 * ML_QK_DIM + ML_V_DIM) // ML_DV
    c_spec = pl.BlockSpec((tb, 1, ML_DQK, ML_DV), lambda i, h: (i, h, 0, 0))
    n_spec = pl.BlockSpec((tb, 1, 1, ML_DQK), lambda i, h: (i, h, 0, 0))
    m_spec = pl.BlockSpec((tb, 1, 1, LANES), lambda i, h: (i, h, 0, 0))
    n4 = n0.reshape(M, ML_HEADS, 1, ML_DQK)
    m4 = jnp.broadcast_to(m0.reshape(M, ML_HEADS, 1, 1), (M, ML_HEADS, 1, LANES))
    return pl.pallas_call(
        _ml_step_body,
        out_shape=(jax.ShapeDtypeStruct((M, ML_V_DIM), BF16), jax.ShapeDtypeStruct(c0.shape, F32),
                   jax.ShapeDtypeStruct(n4.shape, F32), jax.ShapeDtypeStruct(m4.shape, F32)),
        grid=(M // tb, ML_HEADS),
        in_specs=[
            pl.BlockSpec((tb, ML_DQK), lambda i, h: (i, h)),
            pl.BlockSpec((tb, ML_DQK), lambda i, h: (i, kk + h)),
            pl.BlockSpec((tb, ML_DV), lambda i, h: (i, kv + h)),
            pl.BlockSpec((tb, ML_DV), lambda i, h: (i, ko + h)),
            pl.BlockSpec((1, tb, 2), lambda i, h: (h, i, 0)),
            pl.BlockSpec((1, 1, 2), lambda i, h: (h, 0, 0)),
            pl.BlockSpec((1, ML_DV), lambda i, h: (0, h)),
            c_spec, n_spec, m_spec,
        ],
        out_specs=(pl.BlockSpec((tb, ML_DV), lambda i, h: (i, h)), c_spec, n_spec, m_spec),
        compiler_params=_cparams("parallel", "parallel"),
        name="ml_step",
    )(proj, proj, proj, proj, gr, pr, norm_w.reshape(1, ML_V_DIM), c0, n4, m4)


def _last_rows(proj, nseq, seq_len, n, col0, width):
    n = min(n, seq_len)
    return proj.reshape(nseq, seq_len, proj.shape[1])[:, seq_len - n:, col0:col0 + width]


def _trunk(x, pos0, states, p, wb):
    nseq, seq_len, D = x.shape
    M = nseq * seq_len
    seq = states is None
    xs = x.reshape(M, D)
    ffn_bufs = []
    out = {}
    for layer in range(4):
        hn = rmsnorm(xs, p['norm_mix'][layer], BF16)
        if layer == 0:
            proj = matmul(hn, wb['w_ret_in'], name="ret_in")
            if seq:
                y, out['ret'] = ret_prompt(proj, nseq, seq_len)
            else:
                y, out['ret'] = ret_step(proj, states['ret'], pos0)
            xs = matmul(y, wb['w_ret_out'], res=xs, name="ret_out")
        elif layer == 1:
            proj = matmul(hn, wb['w_ssm_in'], name="ssm_in")
            raw = _last_rows(proj, nseq, seq_len, SSM_CONV - 1, SSM_D_INNER, SSM_CONV_DIM)
            if seq:
                xa = conv_seq(proj, SSM_D_INNER, p['ssm_conv_w'], p['ssm_conv_b'], nseq, seq_len)
                out['ssm_conv'] = raw
                y, out['ssm'] = ssm_prompt(proj, xa, p['ssm_dt_bias'], p['ssm_a_log'], p['ssm_d_skip'],
                                           p['ssm_norm_w'], nseq, seq_len)
            else:
                xa = conv_step(proj, SSM_D_INNER, p['ssm_conv_w'], p['ssm_conv_b'], states['ssm_conv'])
                out['ssm_conv'] = jnp.concatenate([states['ssm_conv'][:, 1:], raw], axis=1)
                y, out['ssm'] = ssm_step(proj, xa, states['ssm'], p['ssm_dt_bias'], p['ssm_a_log'], p['ssm_d_skip'],
                                         p['ssm_norm_w'])
            xs = matmul(y, wb['w_ssm_out'], res=xs, name="ssm_out")
        elif layer == 2:
            proj = matmul(hn, wb['w_gdn_in'], name="gdn_in")
            raw = _last_rows(proj, nseq, seq_len, GDN_CONV - 1, 0, GDN_CONV_DIM)
            zero_b = jnp.zeros((GDN_CONV_DIM,), F32)
            if seq:
                qkv = conv_seq(proj, 0, p['gdn_conv_w'], zero_b, nseq, seq_len)
                out['gdn_conv'] = raw
                y, out['gdn'] = gdn_prompt(proj, qkv, p['gdn_a_log'], p['gdn_dt_bias'], p['gdn_norm_w'], nseq, seq_len)
            else:
                qkv = conv_step(proj, 0, p['gdn_conv_w'], zero_b, states['gdn_conv'])
                out['gdn_conv'] = jnp.concatenate([states['gdn_conv'][:, 1:], raw], axis=1)
                y, out['gdn'] = gdn_step(proj, qkv, states['gdn'], p['gdn_a_log'], p['gdn_dt_bias'], p['gdn_norm_w'])
            xs = matmul(y, wb['w_gdn_out'], res=xs, name="gdn_out")
        else:
            proj = matmul(hn, wb['w_ml_in'], name="ml_in")
            if seq:
                y, c, n, m = ml_prompt(proj, p['ml_gate_b'], p['ml_norm_w'], nseq, seq_len)
            else:
                y, c, n, m = ml_step(proj, states['ml_c'], states['ml_n'], states['ml_m'], p['ml_gate_b'],
                                     p['ml_norm_w'])
            out['ml_c'] = c
            out['ml_n'] = n.reshape(nseq, ML_HEADS, ML_DQK)
            out['ml_m'] = m[:, :, 0, 0]
            xs = matmul(y, wb['w_ml_out'], res=xs, name="ml_out")
        hn = rmsnorm(xs, p['norm_ffn'][layer], BF16)
        wg, wu, wd = wb['ffn_w_gate'][layer], wb['ffn_w_up'][layer], wb['ffn_w_down'][layer]
        padf = lambda a: jnp.pad(a, [(0, 0)] * (a.ndim - 1) + [(0, D_FF_PAD - D_FF)])
        cw, cbias = padf(p['ffn_conv_w'][layer]), padf(p['ffn_conv_b'][layer])
        if seq:
            act, tails = ffn1_seq(hn, wg, wu, cw, cbias, seq_len)
            tails = tails.reshape(nseq, -1, SUBLANES, D_FF_PAD)
            ffn_bufs.append(tails[:, -1, SUBLANES - (FFN_CONV - 1):, :D_FF])
        else:
            hist = states['ffn'][layer]
            act, g_raw = ffn1_step(hn, wg, wu, cw, cbias, padf(hist[:, 0]), padf(hist[:, 1]))
            ffn_bufs.append(jnp.stack([hist[:, 1], g_raw[:, :D_FF]], axis=1))
        xs = matmul(act, wd, res=xs, name="ffn_down")
    y = rmsnorm(xs, p['norm_final'], F32).reshape(nseq, seq_len, D)
    return (y, out['ret'], out['ssm'], out['ssm_conv'], out['gdn'], out['gdn_conv'], out['ml_c'], out['ml_n'],
            out['ml_m'], jnp.stack(ffn_bufs))


def kernel(x_prompt, x_sample, state_ret, state_ssm, state_ssm_conv, state_gdn, state_gdn_conv, state_mlstm_c, state_mlstm_n, state_mlstm_m, state_ffn_conv, norm_mix, norm_ffn, norm_final, w_ret_in, w_ret_out, w_ssm_in, ssm_conv_w, ssm_conv_b, ssm_dt_bias, ssm_a_log, ssm_d_skip, ssm_norm_w, w_ssm_out, w_gdn_in, gdn_conv_w, gdn_a_log, gdn_dt_bias, gdn_norm_w, w_gdn_out, w_ml_in, ml_gate_b, ml_norm_w, w_ml_out, ffn_w_gate, ffn_w_up, ffn_conv_w, ffn_conv_b, ffn_w_down):
    p = {
        'norm_mix': norm_mix, 'norm_ffn': norm_ffn, 'norm_final': norm_final,
        'ssm_conv_w': ssm_conv_w, 'ssm_conv_b': ssm_conv_b, 'ssm_dt_bias': ssm_dt_bias, 'ssm_a_log': ssm_a_log,
        'ssm_d_skip': ssm_d_skip, 'ssm_norm_w': ssm_norm_w,
        'gdn_conv_w': gdn_conv_w, 'gdn_a_log': gdn_a_log, 'gdn_dt_bias': gdn_dt_bias, 'gdn_norm_w': gdn_norm_w,
        'ml_gate_b': ml_gate_b, 'ml_norm_w': ml_norm_w,
        'ffn_conv_w': ffn_conv_w, 'ffn_conv_b': ffn_conv_b,
    }
    wb = {
        'w_ret_in': w_ret_in.astype(BF16), 'w_ret_out': w_ret_out.astype(BF16),
        'w_ssm_in': w_ssm_in.astype(BF16), 'w_ssm_out': w_ssm_out.astype(BF16),
        'w_gdn_in': w_gdn_in.astype(BF16), 'w_gdn_out': w_gdn_out.astype(BF16),
        'w_ml_in': w_ml_in.astype(BF16), 'w_ml_out': w_ml_out.astype(BF16),
        'ffn_w_gate': [jnp.pad(ffn_w_gate[l].astype(BF16), ((0, 0), (0, D_FF_PAD - D_FF))) for l in range(4)],
        'ffn_w_up': [jnp.pad(ffn_w_up[l].astype(BF16), ((0, 0), (0, D_FF_PAD - D_FF))) for l in range(4)],
        'ffn_w_down': [jnp.pad(ffn_w_down[l].astype(BF16), ((0, D_FF_PAD - D_FF), (0, 0))) for l in range(4)],
    }
    prompt = _trunk(x_prompt, 0, None, p, wb)
    states = {
        'ret': state_ret, 'ssm': state_ssm, 'ssm_conv': state_ssm_conv, 'gdn': state_gdn, 'gdn_conv': state_gdn_conv,
        'ml_c': state_mlstm_c, 'ml_n': state_mlstm_n, 'ml_m': state_mlstm_m, 'ffn': state_ffn_conv,
    }
    sample = _trunk(x_sample, PAST_LEN, states, p, wb)
    return (prompt[0], sample[0]) + prompt[1:] + sample[1:]
```

```python
import functools

import jax
import jax.numpy as jnp
from jax import lax
from jax.experimental import pallas as pl
from jax.experimental.pallas import tpu as pltpu

F32 = jnp.float32
BF16 = jnp.bfloat16

D_MODEL = 4096
PAST_LEN = 16384
CHUNK = 128
NORM_EPS = 1e-6
ROPE_BASE = 10000.0

RET_HEADS = 16
RET_DK = 256
RET_DV = 512
RET_QK_DIM = RET_HEADS * RET_DK
RET_V_DIM = RET_HEADS * RET_DV

SSM_D_INNER = 8192
SSM_HEAD_DIM = 64
SSM_HEADS = 128
SSM_STATE = 128
SSM_GROUPS = 8
SSM_HPG = SSM_HEADS // SSM_GROUPS
SSM_GROUP_W = SSM_HPG * SSM_HEAD_DIM
SSM_CONV = 4
SSM_CONV_DIM = SSM_D_INNER + 2 * SSM_GROUPS * SSM_STATE

GDN_K_HEADS = 32
GDN_V_HEADS = 64
GDN_REP = GDN_V_HEADS // GDN_K_HEADS
GDN_DK = 128
GDN_DV = 128
GDN_CONV = 4
GDN_KEY_DIM = GDN_K_HEADS * GDN_DK
GDN_VAL_DIM = GDN_V_HEADS * GDN_DV
GDN_CONV_DIM = 2 * GDN_KEY_DIM + GDN_VAL_DIM

ML_HEADS = 8
ML_DQK = 256
ML_DV = 512
ML_QK_DIM = ML_HEADS * ML_DQK
ML_V_DIM = ML_HEADS * ML_DV

D_FF = 11008
FFN_CONV = 3

VMEM_LIMIT_BYTES = 56 * 1024 * 1024
SUBLANES = 8
LANES = 128


def _cparams(*sem):
    return pltpu.CompilerParams(dimension_semantics=sem, vmem_limit_bytes=VMEM_LIMIT_BYTES)


def _bdot(a, b):
    return jnp.dot(a.astype(BF16), b.astype(BF16), preferred_element_type=F32)


def _bdot_nt(a, b):
    return lax.dot_general(a.astype(BF16), b.astype(BF16), (((1,), (1,)), ((), ())), preferred_element_type=F32)


def _bdot_tn(a, b):
    return lax.dot_general(a.astype(BF16), b.astype(BF16), (((0,), (0,)), ((), ())), preferred_element_type=F32)


def _split2(x):
    hi = x.astype(BF16)
    lo = (x - hi.astype(F32)).astype(BF16)
    return hi, lo


def _split3(x):
    hi = x.astype(BF16)
    r = x - hi.astype(F32)
    mid = r.astype(BF16)
    lo = (r - mid.astype(F32)).astype(BF16)
    return hi, mid, lo


def _dot3(a, b):
    ah, al = _split2(a)
    bh, bl = _split2(b)
    d = functools.partial(jnp.dot, preferred_element_type=F32)
    return d(ah, bh) + (d(ah, bl) + d(al, bh))


def _cumsum_rows(x):
    L = x.shape[0]
    tri = (lax.broadcasted_iota(jnp.int32, (L, L), 0) >= lax.broadcasted_iota(jnp.int32, (L, L), 1)).astype(BF16)
    d = functools.partial(jnp.dot, preferred_element_type=F32)
    hi, mid, lo = _split3(x)
    return d(tri, hi) + (d(tri, mid) + d(tri, lo))


def _cumsum_cols(x):
    L = x.shape[1]
    tri = (lax.broadcasted_iota(jnp.int32, (L, L), 0) <= lax.broadcasted_iota(jnp.int32, (L, L), 1)).astype(BF16)
    d = functools.partial(jnp.dot, preferred_element_type=F32)
    hi, mid, lo = _split3(x)
    return d(hi, tri) + (d(mid, tri) + d(lo, tri))


def _lane_bcast_cols(xt, width=LANES):
    n, tb = xt.shape
    shift = width.bit_length() - 1
    sel = ((lax.broadcasted_iota(jnp.int32, (tb, tb * width), 1) >> shift)
           == lax.broadcasted_iota(jnp.int32, (tb, tb * width), 0)).astype(BF16)
    d = functools.partial(jnp.dot, preferred_element_type=F32)
    hi, mid, lo = _split3(xt)
    return d(hi, sel) + (d(mid, sel) + d(lo, sel))


def _softplus(x):
    return jnp.maximum(x, 0.0) + jnp.log1p(jnp.exp(-jnp.abs(x)))


def _silu(x):
    return x * jax.nn.sigmoid(x)


def _rms(x):
    return x * lax.rsqrt(jnp.mean(x * x, axis=-1, keepdims=True) + NORM_EPS)


def _tri_masks(L):
    ri = lax.broadcasted_iota(jnp.int32, (L, L), 0)
    ci = lax.broadcasted_iota(jnp.int32, (L, L), 1)
    return ri, ci


def _rmsnorm_body(x_ref, w_ref, o_ref):
    o_ref[...] = (_rms(x_ref[...]) * w_ref[...]).astype(o_ref.dtype)


def rmsnorm(x, w, out_dtype):
    M, D = x.shape
    tm = min(M, 256)
    return pl.pallas_call(
        _rmsnorm_body,
        out_shape=jax.ShapeDtypeStruct((M, D), out_dtype),
        grid=(M // tm,),
        in_specs=[pl.BlockSpec((tm, D), lambda i: (i, 0)), pl.BlockSpec((1, D), lambda i: (0, 0))],
        out_specs=pl.BlockSpec((tm, D), lambda i: (i, 0)),
        compiler_params=_cparams("parallel"),
        name="rmsnorm",
    )(x, w.reshape(1, D))


def _mm_body(a_ref, b_ref, o_ref):
    o_ref[...] = jnp.dot(a_ref[...], b_ref[...], preferred_element_type=F32)


def _mm_res_body(a_ref, b_ref, r_ref, o_ref):
    o_ref[...] = r_ref[...] + jnp.dot(a_ref[...], b_ref[...], preferred_element_type=F32)


def _mm_tiles(M, K):
    if M <= 128:
        return M, (1024 if K <= 4096 else 512), False
    tm = min(M, 1024)
    if K <= 4096:
        return tm, 512, False
    if K <= 8192:
        return tm, 256, False
    return tm, 256, True


def _w_spec(w, layer, rows, tn, index):
    if w.ndim == 2:
        return pl.BlockSpec((rows, tn), lambda *g: (0, index(*g)))
    return pl.BlockSpec((None, rows, tn), lambda *g: (layer, 0, index(*g)))


def matmul(a, b, res=None, layer=None, name="matmul"):
    M, K = a.shape
    N = b.shape[-1]
    tm, tn, single = _mm_tiles(M, K)
    tn = min(tn, N)
    a_spec = pl.BlockSpec((tm, K), lambda i, j: (i, 0), **({"pipeline_mode": pl.Buffered(1)} if single else {}))
    in_specs = [a_spec, _w_spec(b, layer, K, tn, lambda i, j: j)]
    args = [a, b]
    body = _mm_body
    if res is not None:
        in_specs.append(pl.BlockSpec((tm, tn), lambda i, j: (i, j)))
        args.append(res)
        body = _mm_res_body
    return pl.pallas_call(
        body,
        out_shape=jax.ShapeDtypeStruct((M, N), F32),
        grid=(M // tm, pl.cdiv(N, tn)),
        in_specs=in_specs,
        out_specs=pl.BlockSpec((tm, tn), lambda i, j: (i, j)),
        compiler_params=_cparams("parallel", "parallel"),
        name=name,
    )(*args)


FFN_TC = 256
FFN_TN = 2 * FFN_TC


def _ffn1_seq_body(x_ref, wg_ref, wu_ref, cw_ref, cb_ref, act_ref, tail_ref, carry_ref, gbuf_ref, *, tiles_per_seq):
    i = pl.program_id(0)
    j = pl.program_id(1)
    tm, tn = act_ref.shape
    tc = FFN_TC

    @pl.when((i == 0) & (j == 0))
    def _():
        carry_ref[...] = jnp.zeros(carry_ref.shape, F32)

    x = x_ref[...]
    for c in range(tn // tc):
        cs = slice(c * tc, (c + 1) * tc)
        g = jnp.dot(x, wg_ref[:, cs], preferred_element_type=F32)
        u = jnp.dot(x, wu_ref[:, cs], preferred_element_type=F32)
        gbuf_ref[c, 0:SUBLANES, :] = jnp.where(i % tiles_per_seq == 0, 0.0, carry_ref[j, :, cs])
        gbuf_ref[c, SUBLANES:SUBLANES + tm, :] = g
        w = cw_ref[:, cs]
        y = (gbuf_ref[c, SUBLANES - 2:SUBLANES - 2 + tm, :] * w[0:1]
             + gbuf_ref[c, SUBLANES - 1:SUBLANES - 1 + tm, :] * w[1:2] + g * w[2:3] + cb_ref[:, cs])
        act_ref[:, cs] = (_silu(y) * u).astype(act_ref.dtype)
        tail = g[tm - SUBLANES:tm, :]
        carry_ref[j, :, cs] = tail
        tail_ref[0, :, cs] = tail


def ffn1_seq(x, wg, wu, cw, cb, seq_len, layer=None):
    M, D = x.shape
    F = wg.shape[-1]
    tm = min(seq_len, 1024)
    tn = FFN_TN
    nj = pl.cdiv(F, tn)
    return pl.pallas_call(
        functools.partial(_ffn1_seq_body, tiles_per_seq=seq_len // tm),
        out_shape=(jax.ShapeDtypeStruct((M, F), BF16), jax.ShapeDtypeStruct((M // tm, SUBLANES, F), F32)),
        grid=(M // tm, nj),
        in_specs=[
            pl.BlockSpec((tm, D), lambda i, j: (i, 0)),
            _w_spec(wg, layer, D, tn, lambda i, j: j),
            _w_spec(wu, layer, D, tn, lambda i, j: j),
            pl.BlockSpec((FFN_CONV, tn), lambda i, j: (0, j)),
            pl.BlockSpec((1, tn), lambda i, j: (0, j)),
        ],
        out_specs=(pl.BlockSpec((tm, tn), lambda i, j: (i, j)), pl.BlockSpec((1, SUBLANES, tn), lambda i, j: (i, 0, j))),
        scratch_shapes=[pltpu.VMEM((nj, SUBLANES, tn), F32), pltpu.VMEM((tn // FFN_TC, tm + SUBLANES, FFN_TC), F32)],
        compiler_params=_cparams("arbitrary", "arbitrary"),
        name="ffn1_seq",
    )(x, wg, wu, cw, cb.reshape(1, F))


def _ffn1_step_body(x_ref, wg_ref, wu_ref, cw_ref, cb_ref, s0_ref, s1_ref, act_ref, g_ref):
    x = x_ref[...]
    g = jnp.dot(x, wg_ref[...], preferred_element_type=F32)
    u = jnp.dot(x, wu_ref[...], preferred_element_type=F32)
    w = cw_ref[...]
    y = s0_ref[...] * w[0:1] + s1_ref[...] * w[1:2] + g * w[2:3] + cb_ref[...]
    act_ref[...] = (_silu(y) * u).astype(act_ref.dtype)
    g_ref[...] = g


def ffn1_step(x, wg, wu, cw, cb, s0, s1, layer=None):
    M, D = x.shape
    F = wg.shape[-1]
    tn = FFN_TN
    return pl.pallas_call(
        _ffn1_step_body,
        out_shape=(jax.ShapeDtypeStruct((M, F), BF16), jax.ShapeDtypeStruct((M, F), F32)),
        grid=(pl.cdiv(F, tn),),
        in_specs=[
            pl.BlockSpec((M, D), lambda j: (0, 0)),
            _w_spec(wg, layer, D, tn, lambda j: j),
            _w_spec(wu, layer, D, tn, lambda j: j),
            pl.BlockSpec((FFN_CONV, tn), lambda j: (0, j)),
            pl.BlockSpec((1, tn), lambda j: (0, j)),
            pl.BlockSpec((M, tn), lambda j: (0, j)),
            pl.BlockSpec((M, tn), lambda j: (0, j)),
        ],
        out_specs=(pl.BlockSpec((M, tn), lambda j: (0, j)), pl.BlockSpec((M, tn), lambda j: (0, j))),
        compiler_params=_cparams("parallel"),
        name="ffn1_step",
    )(x, wg, wu, cw, cb.reshape(1, F), s0, s1)


CONV_TC = 1024


def _conv_seq_body(x_ref, w_ref, b_ref, o_ref, buf_ref, *, width):
    t = pl.program_id(2)
    tt = x_ref.shape[0]

    @pl.when(t == 0)
    def _():
        buf_ref[0:SUBLANES, :] = jnp.zeros((SUBLANES, buf_ref.shape[1]), F32)

    x = x_ref[...]
    buf_ref[SUBLANES:SUBLANES + tt, :] = x
    w = w_ref[...]
    y = x * w[width - 1:width]
    for j in range(1, width):
        y = y + buf_ref[SUBLANES - j:SUBLANES - j + tt, :] * w[width - 1 - j:width - j]
    o_ref[...] = _silu(y + b_ref[...])
    buf_ref[0:SUBLANES, :] = x[tt - SUBLANES:tt, :]


def conv_seq(proj, col0, w, b, nseq, seq_len):
    width, C = w.shape
    tc = CONV_TC
    tt = min(seq_len, 512)
    nt = seq_len // tt
    c0 = col0 // tc
    return pl.pallas_call(
        functools.partial(_conv_seq_body, width=width),
        out_shape=jax.ShapeDtypeStruct((nseq * seq_len, C), F32),
        grid=(nseq, C // tc, nt),
        in_specs=[
            pl.BlockSpec((tt, tc), lambda s, c, t: (s * nt + t, c0 + c)),
            pl.BlockSpec((width, tc), lambda s, c, t: (0, c)),
            pl.BlockSpec((1, tc), lambda s, c, t: (0, c)),
        ],
        out_specs=pl.BlockSpec((tt, tc), lambda s, c, t: (s * nt + t, c)),
        scratch_shapes=[pltpu.VMEM((tt + SUBLANES, tc), F32)],
        compiler_params=_cparams("parallel", "parallel", "arbitrary"),
        name="conv_seq",
    )(proj, w, b.reshape(1, C))


def _conv_step_body(x_ref, s0_ref, s1_ref, s2_ref, w_ref, b_ref, o_ref):
    w = w_ref[...]
    y = s0_ref[...] * w[0:1] + s1_ref[...] * w[1:2] + s2_ref[...] * w[2:3] + x_ref[...] * w[3:4] + b_ref[...]
    o_ref[...] = _silu(y)


def conv_step(proj, col0, w, b, hist):
    width, C = w.shape
    M = proj.shape[0]
    tc = CONV_TC
    c0 = col0 // tc
    spec = pl.BlockSpec((M, tc), lambda c: (0, c))
    return pl.pallas_call(
        _conv_step_body,
        out_shape=jax.ShapeDtypeStruct((M, C), F32),
        grid=(C // tc,),
        in_specs=[pl.BlockSpec((M, tc), lambda c: (0, c0 + c)), spec, spec, spec,
                  pl.BlockSpec((width, tc), lambda c: (0, c)), pl.BlockSpec((1, tc), lambda c: (0, c))],
        out_specs=spec,
        compiler_params=_cparams("parallel"),
        name="conv_step",
    )(proj, hist[:, 0], hist[:, 1], hist[:, 2], w, b.reshape(1, C))


def _ret_log_gamma(h):
    return jnp.log(1.0 - jnp.exp2(-5.0 - jnp.full((1, 1), h, jnp.int32).astype(F32)))


def _rope(x, cos, sin):
    half = x.shape[-1] // 2
    x1, x2 = x[:, :half], x[:, half:]
    return jnp.concatenate([x1 * cos - x2 * sin, x1 * sin + x2 * cos], axis=-1)


RET_H_STEP = 4


def _ret_chunk_body(q_ref, k_ref, v_ref, g_ref, cos_ref, sin_ref, o_ref, s_ref):
    hg = pl.program_id(1)
    c = pl.program_id(2)
    L = q_ref.shape[0]
    heads = range(RET_H_STEP)

    @pl.when(c == 0)
    def _():
        s_ref[...] = jnp.zeros(s_ref.shape, F32)

    ri, ci = _tri_masks(L)
    rel = (ri - ci).astype(F32)
    idx = lax.broadcasted_iota(jnp.int32, (L, 1), 0).astype(F32)
    cos, sin = cos_ref[...], sin_ref[...]
    lg = [_ret_log_gamma(hg * RET_H_STEP + e) for e in heads]
    q = [_rope(q_ref[:, e * RET_DK:(e + 1) * RET_DK], cos, sin) for e in heads]
    k = [_rope(k_ref[:, e * RET_DK:(e + 1) * RET_DK], cos, sin) * (RET_DK ** -0.5) for e in heads]
    v = [v_ref[:, e * RET_DV:(e + 1) * RET_DV] for e in heads]
    s = [s_ref[0, e] for e in heads]
    sc = [_bdot_nt(q[e], k[e]) * jnp.where(rel >= 0, jnp.exp(lg[e] * jnp.maximum(rel, 0.0)), 0.0) for e in heads]
    o = [_bdot(sc[e], v[e]) + _bdot(q[e], s[e]) * jnp.exp((idx + 1.0) * lg[e]) for e in heads]
    s_new = [jnp.exp(L * lg[e]) * s[e] + _bdot_tn(k[e] * jnp.exp((L - 1.0 - idx) * lg[e]), v[e]) for e in heads]
    for e in heads:
        s_ref[0, e] = s_new[e]
        o_ref[:, e * RET_DV:(e + 1) * RET_DV] = (
            _silu(g_ref[:, e * RET_DV:(e + 1) * RET_DV]) * _rms(o[e])).astype(o_ref.dtype)


def _rope_tables(pos):
    half = RET_DK // 2
    inv_freq = 1.0 / (ROPE_BASE ** jnp.linspace(0.0, 1.0, half, dtype=F32))
    ang = pos.astype(F32)[:, None] * inv_freq[None, :]
    return jnp.cos(ang), jnp.sin(ang)


def ret_prompt(proj, nseq, seq_len):
    L = min(CHUNK, seq_len)
    nc = seq_len // L
    M = nseq * seq_len
    cos, sin = _rope_tables(jnp.arange(seq_len))
    hs = RET_H_STEP
    KW, VW = hs * RET_DK, hs * RET_DV
    kq, kv, kg = RET_QK_DIM // KW, (2 * RET_QK_DIM) // VW, (2 * RET_QK_DIM + RET_V_DIM) // VW
    return pl.pallas_call(
        _ret_chunk_body,
        out_shape=(jax.ShapeDtypeStruct((M, RET_V_DIM), BF16),
                   jax.ShapeDtypeStruct((nseq, RET_HEADS, RET_DK, RET_DV), F32)),
        grid=(nseq, RET_HEADS // hs, nc),
        in_specs=[
            pl.BlockSpec((L, KW), lambda b, h, c: (b * nc + c, h)),
            pl.BlockSpec((L, KW), lambda b, h, c: (b * nc + c, kq + h)),
            pl.BlockSpec((L, VW), lambda b, h, c: (b * nc + c, kv + h)),
            pl.BlockSpec((L, VW), lambda b, h, c: (b * nc + c, kg + h)),
            pl.BlockSpec((L, RET_DK // 2), lambda b, h, c: (c, 0)),
            pl.BlockSpec((L, RET_DK // 2), lambda b, h, c: (c, 0)),
        ],
        out_specs=(pl.BlockSpec((L, VW), lambda b, h, c: (b * nc + c, h)),
                   pl.BlockSpec((1, hs, RET_DK, RET_DV), lambda b, h, c: (b, h, 0, 0))),
        compiler_params=_cparams("parallel", "parallel", "arbitrary"),
        name="ret_chunk",
    )(proj, proj, proj, proj, cos, sin)


STEP_TB = 8


def _ret_step_body(q_ref, k_ref, v_ref, g_ref, cos_ref, sin_ref, s_ref, o_ref, so_ref):
    h = pl.program_id(1)
    tb = q_ref.shape[0]
    gamma = jnp.exp(_ret_log_gamma(h))
    cos, sin = cos_ref[...], sin_ref[...]
    q = _rope(q_ref[...], cos, sin)
    k = _rope(k_ref[...], cos, sin) * (RET_DK ** -0.5)
    v = v_ref[...]
    g = g_ref[...]
    sc = jnp.sum(q * k, axis=-1, keepdims=True)
    qt = q.T
    kt = k.T
    for b in range(tb):
        s = s_ref[b, 0]
        vb = v[b:b + 1, :]
        qs = jnp.sum(s * qt[:, b:b + 1], axis=0, keepdims=True)
        o = sc[b:b + 1, :] * vb + qs * gamma
        so_ref[b, 0] = gamma * s + kt[:, b:b + 1] * vb
        o_ref[b:b + 1, :] = (_silu(g[b:b + 1, :]) * _rms(o)).astype(o_ref.dtype)


def ret_step(proj, state, pos):
    M = proj.shape[0]
    tb = STEP_TB
    cos, sin = _rope_tables(jnp.full((1,), pos))
    kq, kv, kg = RET_QK_DIM // RET_DK, (2 * RET_QK_DIM) // RET_DV, (2 * RET_QK_DIM + RET_V_DIM) // RET_DV
    s_spec = pl.BlockSpec((tb, 1, RET_DK, RET_DV), lambda i, h: (i, h, 0, 0))
    return pl.pallas_call(
        _ret_step_body,
        out_shape=(jax.ShapeDtypeStruct((M, RET_V_DIM), BF16), jax.ShapeDtypeStruct(state.shape, F32)),
        grid=(M // tb, RET_HEADS),
        in_specs=[
            pl.BlockSpec((tb, RET_DK), lambda i, h: (i, h)),
            pl.BlockSpec((tb, RET_DK), lambda i, h: (i, kq + h)),
            pl.BlockSpec((tb, RET_DV), lambda i, h: (i, kv + h)),
            pl.BlockSpec((tb, RET_DV), lambda i, h: (i, kg + h)),
            pl.BlockSpec((1, RET_DK // 2), lambda i, h: (0, 0)),
            pl.BlockSpec((1, RET_DK // 2), lambda i, h: (0, 0)),
            s_spec,
        ],
        out_specs=(pl.BlockSpec((tb, RET_DV), lambda i, h: (i, h)), s_spec),
        compiler_params=_cparams("parallel", "parallel"),
        name="ret_step",
    )(proj, proj, proj, proj, cos, sin, state)


def _ssm_chunk_body(x_ref, b_ref, c_ref, z_ref, dtr_ref, dtc_ref, pr_ref, pc_ref, dskip_ref, nw_ref, y_ref, s_ref):
    c = pl.program_id(2)
    L = x_ref.shape[0]
    P2 = 2 * SSM_HEAD_DIM

    @pl.when(c == 0)
    def _():
        s_ref[...] = jnp.zeros(s_ref.shape, F32)

    pr = pr_ref[0]
    pc = pc_ref[0]
    dt_r = _softplus(dtr_ref[0] + pr[0:1, :])
    dt_c = _softplus(dtc_ref[0] + pc[:, 0:1])
    acs_r = _cumsum_rows(dt_r * (-jnp.exp(pr[1:2, :])))
    acs_c = _cumsum_cols(dt_c * (-jnp.exp(pc[:, 1:2])))
    bm = b_ref[...]
    cm = c_ref[...]
    x = x_ref[...]
    cb = _bdot_nt(cm, bm)
    ri, ci = _tri_masks(L)
    incl = ri >= ci
    lo_lane = lax.broadcasted_iota(jnp.int32, (L, P2), 1) < SSM_HEAD_DIM
    lo_row = lax.broadcasted_iota(jnp.int32, (P2, P2), 0) < SSM_HEAD_DIM
    a_end = acs_r[L - 1:L, :]
    acs_b = _lane_bcast_cols(acs_r, LANES)
    ea2 = _lane_bcast_cols(jnp.exp(acs_r), SSM_HEAD_DIM)
    we2 = _lane_bcast_cols(jnp.exp(a_end - acs_r) * dt_r, SSM_HEAD_DIM)
    e_end = jnp.broadcast_to(jnp.exp(acs_c[:, L - 1:L]), (SSM_HPG, P2))
    pairs = range(SSM_HPG // 2)
    sc = [[cb * jnp.exp(jnp.where(incl, acs_b[:, h * LANES:(h + 1) * LANES] - acs_c[h:h + 1, :], -jnp.inf))
           * dt_c[h:h + 1, :] for h in (2 * p, 2 * p + 1)] for p in pairs]
    xp = [x[:, p * P2:(p + 1) * P2] for p in pairs]
    x2 = [jnp.concatenate([jnp.where(lo_lane, xp[p], 0.0), jnp.where(lo_lane, 0.0, xp[p])], axis=0) for p in pairs]
    st = [s_ref[0, 2 * p:2 * p + 2].reshape(P2, SSM_STATE) for p in pairs]
    ys = [_bdot(jnp.concatenate(sc[p], axis=1), x2[p]) + _bdot_nt(cm, st[p]) * ea2[:, p * P2:(p + 1) * P2]
          for p in pairs]
    st_new = [jnp.where(lo_row, e_end[2 * p:2 * p + 1, :], e_end[2 * p + 1:2 * p + 2, :]) * st[p]
              + _bdot_tn(xp[p] * we2[:, p * P2:(p + 1) * P2], bm) for p in pairs]
    for p in pairs:
        s_ref[0, 2 * p:2 * p + 2] = st_new[p].reshape(2, SSM_HEAD_DIM, SSM_STATE)
    y = jnp.concatenate(ys, axis=1) + dskip_ref[...] * x
    y = y * _silu(z_ref[...])
    y_ref[...] = (_rms(y) * nw_ref[...]).astype(y_ref.dtype)


def _ssm_params(dt_bias, a_log):
    pr = jnp.stack([dt_bias.reshape(SSM_GROUPS, SSM_HPG), a_log.reshape(SSM_GROUPS, SSM_HPG)], axis=1)
    return pr, jnp.swapaxes(pr, 1, 2)


def ssm_prompt(proj, xa, dt_bias, a_log, d_skip, norm_w, nseq, seq_len):
    L = min(CHUNK, seq_len)
    nc = seq_len // L
    M = nseq * seq_len
    W = SSM_GROUP_W
    dt_raw = proj[:, SSM_D_INNER + SSM_CONV_DIM:]
    dtr = jnp.transpose(dt_raw.reshape(M, SSM_GROUPS, SSM_HPG), (1, 0, 2))
    dtc = jnp.swapaxes(dtr, 1, 2)
    pr, pc = _ssm_params(dt_bias, a_log)
    dskip = jnp.repeat(d_skip, SSM_HEAD_DIM).reshape(1, SSM_D_INNER)
    kb = SSM_D_INNER // SSM_STATE
    return pl.pallas_call(
        _ssm_chunk_body,
        out_shape=(jax.ShapeDtypeStruct((M, SSM_D_INNER), BF16),
                   jax.ShapeDtypeStruct((nseq, SSM_HEADS, SSM_HEAD_DIM, SSM_STATE), F32)),
        grid=(nseq, SSM_GROUPS, nc),
        in_specs=[
            pl.BlockSpec((L, W), lambda b, g, c: (b * nc + c, g)),
            pl.BlockSpec((L, SSM_STATE), lambda b, g, c: (b * nc + c, kb + g)),
            pl.BlockSpec((L, SSM_STATE), lambda b, g, c: (b * nc + c, kb + SSM_GROUPS + g)),
            pl.BlockSpec((L, W), lambda b, g, c: (b * nc + c, g)),
            pl.BlockSpec((1, L, SSM_HPG), lambda b, g, c: (g, b * nc + c, 0)),
            pl.BlockSpec((1, SSM_HPG, L), lambda b, g, c: (g, 0, b * nc + c)),
            pl.BlockSpec((1, 2, SSM_HPG), lambda b, g, c: (g, 0, 0)),
            pl.BlockSpec((1, SSM_HPG, 2), lambda b, g, c: (g, 0, 0)),
            pl.BlockSpec((1, W), lambda b, g, c: (0, g)),
            pl.BlockSpec((1, W), lambda b, g, c: (0, g)),
        ],
        out_specs=(pl.BlockSpec((L, W), lambda b, g, c: (b * nc + c, g)),
                   pl.BlockSpec((1, SSM_HPG, SSM_HEAD_DIM, SSM_STATE), lambda b, g, c: (b, g, 0, 0))),
        compiler_params=_cparams("parallel", "parallel", "arbitrary"),
        name="ssm_chunk",
    )(xa, xa, xa, proj, dtr, dtc, pr, pc, dskip, norm_w.reshape(1, SSM_D_INNER))


def _ssm_step_body(x_ref, b_ref, c_ref, z_ref, dtx_ref, bias_ref, alog_ref, dskip_ref, nw_ref, s_ref, y_ref, so_ref):
    tb = x_ref.shape[0]
    W = SSM_GROUP_W
    x = x_ref[...]
    bm = b_ref[...]
    cm = c_ref[...]
    dt = _softplus(dtx_ref[...] + bias_ref[...])
    da = jnp.exp(dt * (-jnp.exp(alog_ref[...])))
    cb = jnp.sum(cm * bm, axis=-1, keepdims=True)
    xdt = (x * dt).T
    dat = da.T
    rows = lax.broadcasted_iota(jnp.int32, (tb, W), 0)
    cs = jnp.zeros((tb, W), F32)
    for b in range(tb):
        st = s_ref[b].reshape(W, SSM_STATE)
        cs = jnp.where(rows == b, _bdot_nt(cm, st), cs)
        st_new = dat[:, b:b + 1] * st + xdt[:, b:b + 1] * bm[b:b + 1, :]
        so_ref[b] = st_new.reshape(SSM_HPG, SSM_HEAD_DIM, SSM_STATE)
    y = cb * dt * x + cs * da + dskip_ref[...] * x
    y = y * _silu(z_ref[...])
    y_ref[...] = (_rms(y) * nw_ref[...]).astype(y_ref.dtype)


def ssm_step(proj, xa, state, dt_bias, a_log, d_skip, norm_w):
    M = proj.shape[0]
    tb = STEP_TB
    W = SSM_GROUP_W
    rep = lambda a: jnp.repeat(a, SSM_HEAD_DIM, axis=-1)
    dtx = rep(proj[:, SSM_D_INNER + SSM_CONV_DIM:])
    kb = SSM_D_INNER // SSM_STATE
    s_spec = pl.BlockSpec((tb, SSM_HPG, SSM_HEAD_DIM, SSM_STATE), lambda g, i: (i, g, 0, 0))
    row = pl.BlockSpec((1, W), lambda g, i: (0, g))
    tile = pl.BlockSpec((tb, W), lambda g, i: (i, g))
    return pl.pallas_call(
        _ssm_step_body,
        out_shape=(jax.ShapeDtypeStruct((M, SSM_D_INNER), BF16), jax.ShapeDtypeStruct(state.shape, F32)),
        grid=(SSM_GROUPS, M // tb),
        in_specs=[
            tile,
            pl.BlockSpec((tb, SSM_STATE), lambda g, i: (i, kb + g)),
            pl.BlockSpec((tb, SSM_STATE), lambda g, i: (i, kb + SSM_GROUPS + g)),
            tile, tile, row, row, row, row, s_spec,
        ],
        out_specs=(tile, s_spec),
        compiler_params=_cparams("parallel", "parallel"),
        name="ssm_step",
    )(xa, xa, xa, proj, dtx, rep(dt_bias).reshape(1, -1), rep(a_log).reshape(1, -1), rep(d_skip).reshape(1, -1),
      norm_w.reshape(1, -1), state)


def _l2n(x):
    return x * lax.rsqrt(jnp.sum(x * x, axis=-1, keepdims=True) + NORM_EPS)


def _unit_lower_inverses(mats, ri, ci):
    L = mats[0].shape[0]
    eye = (ri == ci).astype(F32)
    same8 = (ri >> 3) == (ci >> 3)
    ns = [jnp.where(same8, -a, 0.0) for a in mats]
    n2 = [_bdot(n, n) for n in ns]
    n4 = [_bdot(x, x) for x in n2]
    ts = [eye + n for n in ns]
    ts = [t + _bdot(t, x) for t, x in zip(ts, n2)]
    ts = [t + _bdot(t, x) for t, x in zip(ts, n4)]
    sh = 3
    while (1 << sh) < L:
        lower_left = ((ri >> (sh + 1)) == (ci >> (sh + 1))) & ((ri >> sh) != (ci >> sh))
        us = [_bdot(jnp.where(lower_left, a, 0.0), t) for a, t in zip(mats, ts)]
        ts = [t - _bdot(t, u) for t, u in zip(ts, us)]
        sh += 1
    return ts


def _gdn_chunk_body(q_ref, k_ref, v_ref, z_ref, gr_ref, gc_ref, pr_ref, pc_ref, nw_ref, o_ref, s_ref, *, ks):
    c = pl.program_id(2)
    L = q_ref.shape[0]
    vs = ks * GDN_REP

    @pl.when(c == 0)
    def _():
        s_ref[...] = jnp.zeros(s_ref.shape, F32)

    gr = gr_ref[0]
    gc = gc_ref[0]
    pr = pr_ref[0]
    pc = pc_ref[0]
    beta = jax.nn.sigmoid(gr[:, 0:vs])
    gt_r = _cumsum_rows(-jnp.exp(pr[0:1, :]) * _softplus(gr[:, vs:] + pr[1:2, :]))
    gt_c = _cumsum_cols(-jnp.exp(pc[:, 0:1]) * _softplus(gc[vs:, :] + pc[:, 1:2]))
    ri, ci = _tri_masks(L)
    nw = nw_ref[...]
    qs, ks_ = [], []
    for kh in range(ks):
        qs.append(_l2n(q_ref[:, kh * GDN_DK:(kh + 1) * GDN_DK]) * (GDN_DK ** -0.5))
        ks_.append(_l2n(k_ref[:, kh * GDN_DK:(kh + 1) * GDN_DK]))
    kks = [_bdot_nt(k, k) for k in ks_]
    qks = [_bdot_nt(q, k) for q, k in zip(qs, ks_)]
    heads = range(vs)
    kof = [j // GDN_REP for j in heads]
    st = [s_ref[0, j] for j in heads]
    gcol = [gt_r[:, j:j + 1] for j in heads]
    seg = [gcol[j] - gt_c[j:j + 1, :] for j in heads]
    bcol = [beta[:, j:j + 1] for j in heads]
    eg = [jnp.exp(gcol[j]) for j in heads]
    a_mats = [kks[kof[j]] * jnp.exp(jnp.where(ri > ci, seg[j], -jnp.inf)) * bcol[j] for j in heads]
    rhs = [bcol[j] * (v_ref[:, j * GDN_DV:(j + 1) * GDN_DV] - eg[j] * _bdot(ks_[kof[j]], st[j])) for j in heads]
    tinv = _unit_lower_inverses(a_mats, ri, ci)
    delta = [_bdot(tinv[j], rhs[j]) for j in heads]
    o = [eg[j] * _bdot(qs[kof[j]], st[j])
         + _bdot(qks[kof[j]] * jnp.exp(jnp.where(ri >= ci, seg[j], -jnp.inf)), delta[j]) for j in heads]
    g_end = [gt_r[L - 1:L, j:j + 1] for j in heads]
    s_new = [jnp.exp(g_end[j]) * st[j] + _bdot_tn(ks_[kof[j]] * jnp.exp(g_end[j] - gcol[j]), delta[j]) for j in heads]
    for j in heads:
        s_ref[0, j] = s_new[j]
        ze = z_ref[:, j * GDN_DV:(j + 1) * GDN_DV]
        o_ref[:, j * GDN_DV:(j + 1) * GDN_DV] = (_rms(o[j]) * nw * _silu(ze)).astype(o_ref.dtype)


def _gdn_gate_inputs(proj, ks):
    M = proj.shape[0]
    ng, vs = GDN_K_HEADS // ks, ks * GDN_REP
    tail = proj[:, GDN_CONV_DIM + GDN_VAL_DIM:]
    braw = tail[:, :GDN_V_HEADS].reshape(M, ng, vs)
    araw = tail[:, GDN_V_HEADS:].reshape(M, ng, vs)
    gr = jnp.transpose(jnp.concatenate([braw, araw], axis=-1), (1, 0, 2))
    return gr, jnp.swapaxes(gr, 1, 2)


def _gdn_params(a_log, dt_bias, ks):
    ng, vs = GDN_K_HEADS // ks, ks * GDN_REP
    pr = jnp.stack([a_log.reshape(ng, vs), dt_bias.reshape(ng, vs)], axis=1)
    return pr, jnp.swapaxes(pr, 1, 2)


GDN_KH_STEP = 8


def gdn_prompt(proj, qkv, a_log, dt_bias, norm_w, nseq, seq_len):
    L = min(CHUNK, seq_len)
    nc = seq_len // L
    M = nseq * seq_len
    ks = GDN_KH_STEP
    ng, vs = GDN_K_HEADS // ks, ks * GDN_REP
    KW, VW = ks * GDN_DK, vs * GDN_DV
    gr, gc = _gdn_gate_inputs(proj, ks)
    pr, pc = _gdn_params(a_log, dt_bias, ks)
    kk, kv, kz = GDN_KEY_DIM // KW, (2 * GDN_KEY_DIM) // VW, GDN_CONV_DIM // VW
    return pl.pallas_call(
        functools.partial(_gdn_chunk_body, ks=ks),
        out_shape=(jax.ShapeDtypeStruct((M, GDN_VAL_DIM), BF16),
                   jax.ShapeDtypeStruct((nseq, GDN_V_HEADS, GDN_DK, GDN_DV), F32)),
        grid=(nseq, ng, nc),
        in_specs=[
            pl.BlockSpec((L, KW), lambda b, h, c: (b * nc + c, h)),
            pl.BlockSpec((L, KW), lambda b, h, c: (b * nc + c, kk + h)),
            pl.BlockSpec((L, VW), lambda b, h, c: (b * nc + c, kv + h)),
            pl.BlockSpec((L, VW), lambda b, h, c: (b * nc + c, kz + h)),
            pl.BlockSpec((1, L, 2 * vs), lambda b, h, c: (h, b * nc + c, 0)),
            pl.BlockSpec((1, 2 * vs, L), lambda b, h, c: (h, 0, b * nc + c)),
            pl.BlockSpec((1, 2, vs), lambda b, h, c: (h, 0, 0)),
            pl.BlockSpec((1, vs, 2), lambda b, h, c: (h, 0, 0)),
            pl.BlockSpec((1, GDN_DV), lambda b, h, c: (0, 0)),
        ],
        out_specs=(pl.BlockSpec((L, VW), lambda b, h, c: (b * nc + c, h)),
                   pl.BlockSpec((1, vs, GDN_DK, GDN_DV), lambda b, h, c: (b, h, 0, 0))),
        compiler_params=_cparams("parallel", "parallel", "arbitrary"),
        name="gdn_chunk",
    )(qkv, qkv, qkv, proj, gr, gc, pr, pc, norm_w.reshape(1, GDN_DV))


def _gdn_step_body(q_ref, k_ref, v_ref, z_ref, gr_ref, pr_ref, nw_ref, s_ref, o_ref, so_ref, *, ks):
    tb = q_ref.shape[0]
    vs = ks * GDN_REP
    gr = gr_ref[0]
    pr = pr_ref[0]
    beta = jax.nn.sigmoid(gr[:, 0:vs])
    eg_all = jnp.exp(-jnp.exp(pr[0:1, :]) * _softplus(gr[:, vs:] + pr[1:2, :]))
    nw = nw_ref[...]
    for kh in range(ks):
        q = _l2n(q_ref[:, kh * GDN_DK:(kh + 1) * GDN_DK]) * (GDN_DK ** -0.5)
        k = _l2n(k_ref[:, kh * GDN_DK:(kh + 1) * GDN_DK])
        qk = jnp.sum(q * k, axis=-1, keepdims=True)
        qb = _lane_bcast_cols(q.T)
        kb = _lane_bcast_cols(k.T)
        for b in range(tb):
            kcol = kb[:, b * LANES:(b + 1) * LANES]
            qcol = qb[:, b * LANES:(b + 1) * LANES]
            for e in range(GDN_REP):
                j = kh * GDN_REP + e
                s = s_ref[b, j]
                eg = eg_all[b:b + 1, j:j + 1]
                k_s = jnp.sum(s * kcol, axis=0, keepdims=True)
                q_s = jnp.sum(s * qcol, axis=0, keepdims=True)
                delta = beta[b:b + 1, j:j + 1] * (v_ref[b:b + 1, j * GDN_DV:(j + 1) * GDN_DV] - eg * k_s)
                o = eg * q_s + qk[b:b + 1, :] * delta
                so_ref[b, j] = eg * s + kcol * delta
                ze = z_ref[b:b + 1, j * GDN_DV:(j + 1) * GDN_DV]
                o_ref[b:b + 1, j * GDN_DV:(j + 1) * GDN_DV] = (_rms(o) * nw * _silu(ze)).astype(o_ref.dtype)


GDN_KH_STEP_DECODE = 4


def gdn_step(proj, qkv, state, a_log, dt_bias, norm_w):
    M = proj.shape[0]
    tb = STEP_TB
    ks = GDN_KH_STEP_DECODE
    ng, vs = GDN_K_HEADS // ks, ks * GDN_REP
    KW, VW = ks * GDN_DK, vs * GDN_DV
    gr, _ = _gdn_gate_inputs(proj, ks)
    pr, _ = _gdn_params(a_log, dt_bias, ks)
    kk, kv, kz = GDN_KEY_DIM // KW, (2 * GDN_KEY_DIM) // VW, GDN_CONV_DIM // VW
    s_spec = pl.BlockSpec((tb, vs, GDN_DK, GDN_DV), lambda i, h: (i, h, 0, 0))
    return pl.pallas_call(
        functools.partial(_gdn_step_body, ks=ks),
        out_shape=(jax.ShapeDtypeStruct((M, GDN_VAL_DIM), BF16), jax.ShapeDtypeStruct(state.shape, F32)),
        grid=(M // tb, ng),
        in_specs=[
            pl.BlockSpec((tb, KW), lambda i, h: (i, h)),
            pl.BlockSpec((tb, KW), lambda i, h: (i, kk + h)),
            pl.BlockSpec((tb, VW), lambda i, h: (i, kv + h)),
            pl.BlockSpec((tb, VW), lambda i, h: (i, kz + h)),
            pl.BlockSpec((1, tb, 2 * vs), lambda i, h: (h, i, 0)),
            pl.BlockSpec((1, 2, vs), lambda i, h: (h, 0, 0)),
            pl.BlockSpec((1, GDN_DV), lambda i, h: (0, 0)),
            s_spec,
        ],
        out_specs=(pl.BlockSpec((tb, VW), lambda i, h: (i, h)), s_spec),
        compiler_params=_cparams("parallel", "parallel"),
        name="gdn_step",
    )(qkv, qkv, qkv, proj, gr, pr, norm_w.reshape(1, GDN_DV), state)


def _log_sigmoid(x):
    return -_softplus(-x)


ML_H_STEP = 4


def _ml_chunk_body(q_ref, k_ref, v_ref, og_ref, gr_ref, gc_ref, pr_ref, pc_ref, nw_ref, o_ref, c_ref, n_ref, m_ref):
    ch = pl.program_id(2)
    L = q_ref.shape[0]

    @pl.when(ch == 0)
    def _():
        c_ref[...] = jnp.zeros(c_ref.shape, F32)
        n_ref[...] = jnp.zeros(n_ref.shape, F32)
        m_ref[...] = jnp.zeros(m_ref.shape, F32)

    ri, ci = _tri_masks(L)
    H = range(ML_H_STEP)
    q = [q_ref[:, e * ML_DQK:(e + 1) * ML_DQK] for e in H]
    k = [k_ref[:, e * ML_DQK:(e + 1) * ML_DQK] * (ML_DQK ** -0.5) for e in H]
    v = [v_ref[:, e * ML_DV:(e + 1) * ML_DV] for e in H]
    gr = [gr_ref[e] + pr_ref[e] for e in H]
    gc = [gc_ref[e] + pc_ref[e] for e in H]
    it_c = [gr[e][:, 0:1] for e in H]
    it_r = [gc[e][0:1, :] for e in H]
    ft_c = [_cumsum_rows(_log_sigmoid(gr[e][:, 1:2])) for e in H]
    ft_r = [_cumsum_cols(_log_sigmoid(gc[e][1:2, :])) for e in H]
    c_st = [c_ref[0, e] for e in H]
    n_st = [n_ref[0, e] for e in H]
    m_st = [m_ref[0, e][:, 0:1] for e in H]
    dmat = [jnp.where(ri >= ci, ft_c[e] - ft_r[e] + it_r[e], -jnp.inf) for e in H]
    inter = [ft_c[e] + m_st[e] for e in H]
    mi = [jnp.maximum(inter[e], jnp.max(dmat[e], axis=-1, keepdims=True)) for e in H]
    winter = [jnp.exp(inter[e] - mi[e]) for e in H]
    sc = [_bdot_nt(q[e], k[e]) * jnp.exp(dmat[e] - mi[e]) for e in H]
    num = [winter[e] * _bdot(q[e], c_st[e]) + _bdot(sc[e], v[e]) for e in H]
    den = [winter[e] * jnp.sum(q[e] * n_st[e], axis=-1, keepdims=True) + jnp.sum(sc[e], axis=-1, keepdims=True)
           for e in H]
    hh = [num[e] / jnp.maximum(jnp.abs(den[e]), jnp.exp(-mi[e])) for e in H]
    m_new = [mi[e][L - 1:L, :] for e in H]
    f_end = [ft_c[e][L - 1:L, :] for e in H]
    keep = [jnp.exp(f_end[e] + m_st[e] - m_new[e]) for e in H]
    kw = [k[e] * jnp.exp(f_end[e] - ft_c[e] + it_c[e] - m_new[e]) for e in H]
    c_new = [keep[e] * c_st[e] + _bdot_tn(kw[e], v[e]) for e in H]
    for e in H:
        c_ref[0, e] = c_new[e]
        n_ref[0, e] = keep[e] * n_st[e] + jnp.sum(kw[e], axis=0, keepdims=True)
        m_ref[0, e] = jnp.broadcast_to(m_new[e], (1, LANES))
        vs = slice(e * ML_DV, (e + 1) * ML_DV)
        o_ref[:, vs] = (_rms(hh[e]) * nw_ref[:, vs] * jax.nn.sigmoid(og_ref[:, vs])).astype(o_ref.dtype)


def _ml_gate_inputs(proj, gate_b):
    M = proj.shape[0]
    tail = proj[:, 2 * ML_QK_DIM + 2 * ML_V_DIM:]
    gr = jnp.transpose(tail.reshape(M, 2, ML_HEADS), (2, 0, 1))
    pr = jnp.transpose(gate_b.reshape(2, ML_HEADS), (1, 0)).reshape(ML_HEADS, 1, 2)
    return gr, jnp.swapaxes(gr, 1, 2), pr, jnp.swapaxes(pr, 1, 2)


def ml_prompt(proj, gate_b, norm_w, nseq, seq_len):
    L = min(CHUNK, seq_len)
    nc = seq_len // L
    M = nseq * seq_len
    gr, gc, pr, pc = _ml_gate_inputs(proj, gate_b)
    hs = ML_H_STEP
    KW, VW = hs * ML_DQK, hs * ML_DV
    kk, kv, ko = ML_QK_DIM // KW, (2 * ML_QK_DIM) // VW, (2 * ML_QK_DIM + ML_V_DIM) // VW
    return pl.pallas_call(
        _ml_chunk_body,
        out_shape=(jax.ShapeDtypeStruct((M, ML_V_DIM), BF16),
                   jax.ShapeDtypeStruct((nseq, ML_HEADS, ML_DQK, ML_DV), F32),
                   jax.ShapeDtypeStruct((nseq, ML_HEADS, 1, ML_DQK), F32),
                   jax.ShapeDtypeStruct((nseq, ML_HEADS, 1, LANES), F32)),
        grid=(nseq, ML_HEADS // hs, nc),
        in_specs=[
            pl.BlockSpec((L, KW), lambda b, h, c: (b * nc + c, h)),
            pl.BlockSpec((L, KW), lambda b, h, c: (b * nc + c, kk + h)),
            pl.BlockSpec((L, VW), lambda b, h, c: (b * nc + c, kv + h)),
            pl.BlockSpec((L, VW), lambda b, h, c: (b * nc + c, ko + h)),
            pl.BlockSpec((hs, L, 2), lambda b, h, c: (h, b * nc + c, 0)),
            pl.BlockSpec((hs, 2, L), lambda b, h, c: (h, 0, b * nc + c)),
            pl.BlockSpec((hs, 1, 2), lambda b, h, c: (h, 0, 0)),
            pl.BlockSpec((hs, 2, 1), lambda b, h, c: (h, 0, 0)),
            pl.BlockSpec((1, VW), lambda b, h, c: (0, h)),
        ],
        out_specs=(pl.BlockSpec((L, VW), lambda b, h, c: (b * nc + c, h)),
                   pl.BlockSpec((1, hs, ML_DQK, ML_DV), lambda b, h, c: (b, h, 0, 0)),
                   pl.BlockSpec((1, hs, 1, ML_DQK), lambda b, h, c: (b, h, 0, 0)),
                   pl.BlockSpec((1, hs, 1, LANES), lambda b, h, c: (b, h, 0, 0))),
        compiler_params=_cparams("parallel", "parallel", "arbitrary"),
        name="ml_chunk",
    )(proj, proj, proj, proj, gr, gc, pr, pc, norm_w.reshape(1, ML_V_DIM))


def _ml_step_body(q_ref, k_ref, v_ref, og_ref, gr_ref, pr_ref, nw_ref, c_ref, n_ref, m_ref, o_ref, co_ref, no_ref, mo_ref):
    tb = q_ref.shape[0]
    q = q_ref[...]
    k = k_ref[...] * (ML_DQK ** -0.5)
    v = v_ref[...]
    og = og_ref[...]
    gr = gr_ref[0] + pr_ref[0]
    it = gr[:, 0:1]
    ft = _log_sigmoid(gr[:, 1:2])
    qk = jnp.sum(q * k, axis=-1, keepdims=True)
    qt = _lane_bcast_cols(q.T)
    kt = _lane_bcast_cols(k.T)
    nw = nw_ref[...]
    reps = ML_DV // LANES
    for b in range(tb):
        qcol = jnp.concatenate([qt[:, b * LANES:(b + 1) * LANES]] * reps, axis=1)
        kcol = jnp.concatenate([kt[:, b * LANES:(b + 1) * LANES]] * reps, axis=1)
        c_st = c_ref[b, 0]
        n_st = n_ref[b, 0]
        m_st = m_ref[b, 0][:, 0:1]
        qb = q[b:b + 1, :]
        kb = k[b:b + 1, :]
        vb = v[b:b + 1, :]
        itb = it[b:b + 1, :]
        ftb = ft[b:b + 1, :]
        inter = ftb + m_st
        mi = jnp.maximum(inter, itb)
        wmat = jnp.exp(itb - mi)
        winter = jnp.exp(inter - mi)
        sc = qk[b:b + 1, :] * wmat
        num = winter * jnp.sum(c_st * qcol, axis=0, keepdims=True) + sc * vb
        den = winter * jnp.sum(qb * n_st, axis=-1, keepdims=True) + sc
        hh = num / jnp.maximum(jnp.abs(den), jnp.exp(-mi))
        wk = jnp.exp(itb - mi)
        keep = jnp.exp(ftb + m_st - mi)
        co_ref[b, 0] = keep * c_st + (kcol * wk) * vb
        no_ref[b, 0] = keep * n_st + kb * wk
        mo_ref[b, 0] = jnp.broadcast_to(mi, (1, LANES))
        o_ref[b:b + 1, :] = (_rms(hh) * nw * jax.nn.sigmoid(og[b:b + 1, :])).astype(o_ref.dtype)


def ml_step(proj, c0, n0, m0, gate_b, norm_w):
    M = proj.shape[0]
    tb = STEP_TB
    gr, _, pr, _ = _ml_gate_inputs(proj, gate_b)
    kk, kv, ko = ML_QK_DIM // ML_DQK, (2 * ML_QK_DIM) // ML_DV, (2 * ML_QK_DIM + ML_V_DIM) // ML_DV
    c_spec = pl.BlockSpec((tb, 1, ML_DQK, ML_DV), lambda i, h: (i, h, 0, 0))
    n_spec = pl.BlockSpec((tb, 1, 1, ML_DQK), lambda i, h: (i, h, 0, 0))
    m_spec = pl.BlockSpec((tb, 1, 1, LANES), lambda i, h: (i, h, 0, 0))
    n4 = n0.reshape(M, ML_HEADS, 1, ML_DQK)
    m4 = jnp.broadcast_to(m0.reshape(M, ML_HEADS, 1, 1), (M, ML_HEADS, 1, LANES))
    return pl.pallas_call(
        _ml_step_body,
        out_shape=(jax.ShapeDtypeStruct((M, ML_V_DIM), BF16), jax.ShapeDtypeStruct(c0.shape, F32),
                   jax.ShapeDtypeStruct(n4.shape, F32), jax.ShapeDtypeStruct(m4.shape, F32)),
        grid=(M // tb, ML_HEADS),
        in_specs=[
            pl.BlockSpec((tb, ML_DQK), lambda i, h: (i, h)),
            pl.BlockSpec((tb, ML_DQK), lambda i, h: (i, kk + h)),
            pl.BlockSpec((tb, ML_DV), lambda i, h: (i, kv + h)),
            pl.BlockSpec((tb, ML_DV), lambda i, h: (i, ko + h)),
            pl.BlockSpec((1, tb, 2), lambda i, h: (h, i, 0)),
            pl.BlockSpec((1, 1, 2), lambda i, h: (h, 0, 0)),
            pl.BlockSpec((1, ML_DV), lambda i, h: (0, h)),
            c_spec, n_spec, m_spec,
        ],
        out_specs=(pl.BlockSpec((tb, ML_DV), lambda i, h: (i, h)), c_spec, n_spec, m_spec),
        compiler_params=_cparams("parallel", "parallel"),
        name="ml_step",
    )(proj, proj, proj, proj, gr, pr, norm_w.reshape(1, ML_V_DIM), c0, n4, m4)


def _last_rows(proj, nseq, seq_len, n, col0, width):
    n = min(n, seq_len)
    return proj.reshape(nseq, seq_len, proj.shape[1])[:, seq_len - n:, col0:col0 + width]


def _trunk(x, pos0, states, p, wb):
    nseq, seq_len, D = x.shape
    M = nseq * seq_len
    seq = states is None
    xs = x.reshape(M, D)
    ffn_bufs = []
    out = {}
    for layer in range(4):
        hn = rmsnorm(xs, p['norm_mix'][layer], BF16)
        if layer == 0:
            proj = matmul(hn, wb['w_ret_in'], name="ret_in")
            if seq:
                y, out['ret'] = ret_prompt(proj, nseq, seq_len)
            else:
                y, out['ret'] = ret_step(proj, states['ret'], pos0)
            xs = matmul(y, wb['w_ret_out'], res=xs, name="ret_out")
        elif layer == 1:
            proj = matmul(hn, wb['w_ssm_in'], name="ssm_in")
            raw = _last_rows(proj, nseq, seq_len, SSM_CONV - 1, SSM_D_INNER, SSM_CONV_DIM)
            if seq:
                xa = conv_seq(proj, SSM_D_INNER, p['ssm_conv_w'], p['ssm_conv_b'], nseq, seq_len)
                out['ssm_conv'] = raw
                y, out['ssm'] = ssm_prompt(proj, xa, p['ssm_dt_bias'], p['ssm_a_log'], p['ssm_d_skip'],
                                           p['ssm_norm_w'], nseq, seq_len)
            else:
                xa = conv_step(proj, SSM_D_INNER, p['ssm_conv_w'], p['ssm_conv_b'], states['ssm_conv'])
                out['ssm_conv'] = jnp.concatenate([states['ssm_conv'][:, 1:], raw], axis=1)
                y, out['ssm'] = ssm_step(proj, xa, states['ssm'], p['ssm_dt_bias'], p['ssm_a_log'], p['ssm_d_skip'],
                                         p['ssm_norm_w'])
            xs = matmul(y, wb['w_ssm_out'], res=xs, name="ssm_out")
        elif layer == 2:
            proj = matmul(hn, wb['w_gdn_in'], name="gdn_in")
            raw = _last_rows(proj, nseq, seq_len, GDN_CONV - 1, 0, GDN_CONV_DIM)
            zero_b = jnp.zeros((GDN_CONV_DIM,), F32)
            if seq:
                qkv = conv_seq(proj, 0, p['gdn_conv_w'], zero_b, nseq, seq_len)
                out['gdn_conv'] = raw
                y, out['gdn'] = gdn_prompt(proj, qkv, p['gdn_a_log'], p['gdn_dt_bias'], p['gdn_norm_w'], nseq, seq_len)
            else:
                qkv = conv_step(proj, 0, p['gdn_conv_w'], zero_b, states['gdn_conv'])
                out['gdn_conv'] = jnp.concatenate([states['gdn_conv'][:, 1:], raw], axis=1)
                y, out['gdn'] = gdn_step(proj, qkv, states['gdn'], p['gdn_a_log'], p['gdn_dt_bias'], p['gdn_norm_w'])
            xs = matmul(y, wb['w_gdn_out'], res=xs, name="gdn_out")
        else:
            proj = matmul(hn, wb['w_ml_in'], name="ml_in")
            if seq:
                y, c, n, m = ml_prompt(proj, p['ml_gate_b'], p['ml_norm_w'], nseq, seq_len)
            else:
                y, c, n, m = ml_step(proj, states['ml_c'], states['ml_n'], states['ml_m'], p['ml_gate_b'],
                                     p['ml_norm_w'])
            out['ml_c'] = c
            out['ml_n'] = n.reshape(nseq, ML_HEADS, ML_DQK)
            out['ml_m'] = m[:, :, 0, 0]
            xs = matmul(y, wb['w_ml_out'], res=xs, name="ml_out")
        hn = rmsnorm(xs, p['norm_ffn'][layer], BF16)
        wg, wu, wd = wb['ffn_w_gate'], wb['ffn_w_up'], wb['ffn_w_down']
        cw, cbias = p['ffn_conv_w'][layer], p['ffn_conv_b'][layer]
        if seq:
            act, tails = ffn1_seq(hn, wg, wu, cw, cbias, seq_len, layer=layer)
            tails = tails.reshape(nseq, -1, SUBLANES, D_FF)
            ffn_bufs.append(tails[:, -1, SUBLANES - (FFN_CONV - 1):])
        else:
            hist = states['ffn'][layer]
            act, g_raw = ffn1_step(hn, wg, wu, cw, cbias, hist[:, 0], hist[:, 1], layer=layer)
            ffn_bufs.append(jnp.stack([hist[:, 1], g_raw], axis=1))
        xs = matmul(act, wd, res=xs, layer=layer, name="ffn_down")
    y = rmsnorm(xs, p['norm_final'], F32).reshape(nseq, seq_len, D)
    return (y, out['ret'], out['ssm'], out['ssm_conv'], out['gdn'], out['gdn_conv'], out['ml_c'], out['ml_n'],
            out['ml_m'], jnp.stack(ffn_bufs))


def kernel(x_prompt, x_sample, state_ret, state_ssm, state_ssm_conv, state_gdn, state_gdn_conv, state_mlstm_c, state_mlstm_n, state_mlstm_m, state_ffn_conv, norm_mix, norm_ffn, norm_final, w_ret_in, w_ret_out, w_ssm_in, ssm_conv_w, ssm_conv_b, ssm_dt_bias, ssm_a_log, ssm_d_skip, ssm_norm_w, w_ssm_out, w_gdn_in, gdn_conv_w, gdn_a_log, gdn_dt_bias, gdn_norm_w, w_gdn_out, w_ml_in, ml_gate_b, ml_norm_w, w_ml_out, ffn_w_gate, ffn_w_up, ffn_conv_w, ffn_conv_b, ffn_w_down):
    p = {
        'norm_mix': norm_mix, 'norm_ffn': norm_ffn, 'norm_final': norm_final,
        'ssm_conv_w': ssm_conv_w, 'ssm_conv_b': ssm_conv_b, 'ssm_dt_bias': ssm_dt_bias, 'ssm_a_log': ssm_a_log,
        'ssm_d_skip': ssm_d_skip, 'ssm_norm_w': ssm_norm_w,
        'gdn_conv_w': gdn_conv_w, 'gdn_a_log': gdn_a_log, 'gdn_dt_bias': gdn_dt_bias, 'gdn_norm_w': gdn_norm_w,
        'ml_gate_b': ml_gate_b, 'ml_norm_w': ml_norm_w,
        'ffn_conv_w': ffn_conv_w, 'ffn_conv_b': ffn_conv_b,
    }
    wb = {
        'w_ret_in': w_ret_in.astype(BF16), 'w_ret_out': w_ret_out.astype(BF16),
        'w_ssm_in': w_ssm_in.astype(BF16), 'w_ssm_out': w_ssm_out.astype(BF16),
        'w_gdn_in': w_gdn_in.astype(BF16), 'w_gdn_out': w_gdn_out.astype(BF16),
        'w_ml_in': w_ml_in.astype(BF16), 'w_ml_out': w_ml_out.astype(BF16),
        'ffn_w_gate': ffn_w_gate.astype(BF16), 'ffn_w_up': ffn_w_up.astype(BF16),
        'ffn_w_down': ffn_w_down.astype(BF16),
    }
    prompt = _trunk(x_prompt, 0, None, p, wb)
    states = {
        'ret': state_ret, 'ssm': state_ssm, 'ssm_conv': state_ssm_conv, 'gdn': state_gdn, 'gdn_conv': state_gdn_conv,
        'ml_c': state_mlstm_c, 'ml_n': state_mlstm_n, 'ml_m': state_mlstm_m, 'ffn': state_ffn_conv,
    }
    sample = _trunk(x_sample, PAST_LEN, states, p, wb)
    return (prompt[0], sample[0]) + prompt[1:] + sample[1:]
```

```python
import functools

import jax
import jax.numpy as jnp
from jax import lax
from jax.experimental import pallas as pl
from jax.experimental.pallas import tpu as pltpu

F32 = jnp.float32
BF16 = jnp.bfloat16

D_MODEL = 4096
PAST_LEN = 16384
CHUNK = 128
NORM_EPS = 1e-6
ROPE_BASE = 10000.0

RET_HEADS = 16
RET_DK = 256
RET_DV = 512
RET_QK_DIM = RET_HEADS * RET_DK
RET_V_DIM = RET_HEADS * RET_DV

SSM_D_INNER = 8192
SSM_HEAD_DIM = 64
SSM_HEADS = 128
SSM_STATE = 128
SSM_GROUPS = 8
SSM_HPG = SSM_HEADS // SSM_GROUPS
SSM_GROUP_W = SSM_HPG * SSM_HEAD_DIM
SSM_CONV = 4
SSM_CONV_DIM = SSM_D_INNER + 2 * SSM_GROUPS * SSM_STATE

GDN_K_HEADS = 32
GDN_V_HEADS = 64
GDN_REP = GDN_V_HEADS // GDN_K_HEADS
GDN_DK = 128
GDN_DV = 128
GDN_CONV = 4
GDN_KEY_DIM = GDN_K_HEADS * GDN_DK
GDN_VAL_DIM = GDN_V_HEADS * GDN_DV
GDN_CONV_DIM = 2 * GDN_KEY_DIM + GDN_VAL_DIM

ML_HEADS = 8
ML_DQK = 256
ML_DV = 512
ML_QK_DIM = ML_HEADS * ML_DQK
ML_V_DIM = ML_HEADS * ML_DV

D_FF = 11008
FFN_CONV = 3

VMEM_LIMIT_BYTES = 56 * 1024 * 1024
SUBLANES = 8
LANES = 128


def _cparams(*sem):
    return pltpu.CompilerParams(dimension_semantics=sem, vmem_limit_bytes=VMEM_LIMIT_BYTES)


def _bdot(a, b):
    return jnp.dot(a.astype(BF16), b.astype(BF16), preferred_element_type=F32)


def _bdot_nt(a, b):
    return lax.dot_general(a.astype(BF16), b.astype(BF16), (((1,), (1,)), ((), ())), preferred_element_type=F32)


def _bdot_tn(a, b):
    return lax.dot_general(a.astype(BF16), b.astype(BF16), (((0,), (0,)), ((), ())), preferred_element_type=F32)


def _split2(x):
    hi = x.astype(BF16)
    lo = (x - hi.astype(F32)).astype(BF16)
    return hi, lo


def _split3(x):
    hi = x.astype(BF16)
    r = x - hi.astype(F32)
    mid = r.astype(BF16)
    lo = (r - mid.astype(F32)).astype(BF16)
    return hi, mid, lo


def _dot3(a, b):
    ah, al = _split2(a)
    bh, bl = _split2(b)
    d = functools.partial(jnp.dot, preferred_element_type=F32)
    return d(ah, bh) + (d(ah, bl) + d(al, bh))


def _cumsum_rows(x):
    L = x.shape[0]
    tri = (lax.broadcasted_iota(jnp.int32, (L, L), 0) >= lax.broadcasted_iota(jnp.int32, (L, L), 1)).astype(BF16)
    d = functools.partial(jnp.dot, preferred_element_type=F32)
    hi, mid, lo = _split3(x)
    return d(tri, hi) + (d(tri, mid) + d(tri, lo))


def _cumsum_cols(x):
    L = x.shape[1]
    tri = (lax.broadcasted_iota(jnp.int32, (L, L), 0) <= lax.broadcasted_iota(jnp.int32, (L, L), 1)).astype(BF16)
    d = functools.partial(jnp.dot, preferred_element_type=F32)
    hi, mid, lo = _split3(x)
    return d(hi, tri) + (d(mid, tri) + d(lo, tri))


def _lane_bcast_cols(xt, width=LANES):
    n, tb = xt.shape
    shift = width.bit_length() - 1
    sel = ((lax.broadcasted_iota(jnp.int32, (tb, tb * width), 1) >> shift)
           == lax.broadcasted_iota(jnp.int32, (tb, tb * width), 0)).astype(BF16)
    d = functools.partial(jnp.dot, preferred_element_type=F32)
    hi, mid, lo = _split3(xt)
    return d(hi, sel) + (d(mid, sel) + d(lo, sel))


def _softplus(x):
    return jnp.maximum(x, 0.0) + jnp.log1p(jnp.exp(-jnp.abs(x)))


def _silu(x):
    return x * jax.nn.sigmoid(x)


def _rms(x):
    return x * lax.rsqrt(jnp.mean(x * x, axis=-1, keepdims=True) + NORM_EPS)


def _tri_masks(L):
    ri = lax.broadcasted_iota(jnp.int32, (L, L), 0)
    ci = lax.broadcasted_iota(jnp.int32, (L, L), 1)
    return ri, ci


def _rmsnorm_body(x_ref, w_ref, o_ref):
    o_ref[...] = (_rms(x_ref[...]) * w_ref[...]).astype(o_ref.dtype)


def rmsnorm(x, w, out_dtype):
    M, D = x.shape
    tm = min(M, 256)
    return pl.pallas_call(
        _rmsnorm_body,
        out_shape=jax.ShapeDtypeStruct((M, D), out_dtype),
        grid=(M // tm,),
        in_specs=[pl.BlockSpec((tm, D), lambda i: (i, 0)), pl.BlockSpec((1, D), lambda i: (0, 0))],
        out_specs=pl.BlockSpec((tm, D), lambda i: (i, 0)),
        compiler_params=_cparams("parallel"),
        name="rmsnorm",
    )(x, w.reshape(1, D))


def _mm_body(a_ref, b_ref, o_ref):
    o_ref[...] = jnp.dot(a_ref[...], b_ref[...], preferred_element_type=F32)


def _mm_res_body(a_ref, b_ref, r_ref, o_ref):
    o_ref[...] = r_ref[...] + jnp.dot(a_ref[...], b_ref[...], preferred_element_type=F32)


def _mm_tiles(M, K):
    if M <= 128:
        return M, (1024 if K <= 4096 else 512), False
    tm = min(M, 1024)
    if K <= 4096:
        return tm, 512, False
    if K <= 8192:
        return tm, 256, False
    return tm, 256, True


def _w_spec(w, layer, rows, tn, index):
    if w.ndim == 2:
        return pl.BlockSpec((rows, tn), lambda *g: (0, index(*g)))
    return pl.BlockSpec((None, rows, tn), lambda *g: (layer, 0, index(*g)))


def matmul(a, b, res=None, layer=None, name="matmul"):
    M, K = a.shape
    N = b.shape[-1]
    tm, tn, single = _mm_tiles(M, K)
    tn = min(tn, N)
    a_spec = pl.BlockSpec((tm, K), lambda i, j: (i, 0), **({"pipeline_mode": pl.Buffered(1)} if single else {}))
    in_specs = [a_spec, _w_spec(b, layer, K, tn, lambda i, j: j)]
    args = [a, b]
    body = _mm_body
    if res is not None:
        in_specs.append(pl.BlockSpec((tm, tn), lambda i, j: (i, j)))
        args.append(res)
        body = _mm_res_body
    return pl.pallas_call(
        body,
        out_shape=jax.ShapeDtypeStruct((M, N), F32),
        grid=(M // tm, pl.cdiv(N, tn)),
        in_specs=in_specs,
        out_specs=pl.BlockSpec((tm, tn), lambda i, j: (i, j)),
        compiler_params=_cparams("parallel", "parallel"),
        name=name,
    )(*args)


def _mm_cast_body(a_ref, w_ref, o_ref, wb_ref):
    wt = w_ref[...].astype(BF16)
    wb_ref[...] = wt
    o_ref[...] = jnp.dot(a_ref[...], wt, preferred_element_type=F32)


def _mm_cast_res_body(a_ref, w_ref, r_ref, o_ref, wb_ref):
    wt = w_ref[...].astype(BF16)
    wb_ref[...] = wt
    o_ref[...] = r_ref[...] + jnp.dot(a_ref[...], wt, preferred_element_type=F32)


def matmul_cast(a, w, res=None, layer=None, name="matmul_cast"):
    M, K = a.shape
    N = w.shape[-1]
    tn = 512 if K <= 4096 else 256
    in_specs = [pl.BlockSpec((M, K), lambda j: (0, 0)), _w_spec(w, layer, K, tn, lambda j: j)]
    args = [a, w]
    body = _mm_cast_body
    if res is not None:
        in_specs.append(pl.BlockSpec((M, tn), lambda j: (0, j)))
        args.append(res)
        body = _mm_cast_res_body
    return pl.pallas_call(
        body,
        out_shape=(jax.ShapeDtypeStruct((M, N), F32), jax.ShapeDtypeStruct((K, N), BF16)),
        grid=(pl.cdiv(N, tn),),
        in_specs=in_specs,
        out_specs=(pl.BlockSpec((M, tn), lambda j: (0, j)), pl.BlockSpec((K, tn), lambda j: (0, j))),
        compiler_params=_cparams("parallel"),
        name=name,
    )(*args)


FFN_TC = 256
FFN_TN = 2 * FFN_TC


def _ffn1_seq_body(x_ref, wg_ref, wu_ref, cw_ref, cb_ref, act_ref, tail_ref, carry_ref, gbuf_ref, *, tiles_per_seq):
    i = pl.program_id(0)
    j = pl.program_id(1)
    tm, tn = act_ref.shape
    tc = FFN_TC

    @pl.when((i == 0) & (j == 0))
    def _():
        carry_ref[...] = jnp.zeros(carry_ref.shape, F32)

    x = x_ref[...]
    for c in range(tn // tc):
        cs = slice(c * tc, (c + 1) * tc)
        g = jnp.dot(x, wg_ref[:, cs], preferred_element_type=F32)
        u = jnp.dot(x, wu_ref[:, cs], preferred_element_type=F32)
        gbuf_ref[c, 0:SUBLANES, :] = jnp.where(i % tiles_per_seq == 0, 0.0, carry_ref[j, :, cs])
        gbuf_ref[c, SUBLANES:SUBLANES + tm, :] = g
        w = cw_ref[:, cs]
        y = (gbuf_ref[c, SUBLANES - 2:SUBLANES - 2 + tm, :] * w[0:1]
             + gbuf_ref[c, SUBLANES - 1:SUBLANES - 1 + tm, :] * w[1:2] + g * w[2:3] + cb_ref[:, cs])
        act_ref[:, cs] = (_silu(y) * u).astype(act_ref.dtype)
        tail = g[tm - SUBLANES:tm, :]
        carry_ref[j, :, cs] = tail
        tail_ref[0, :, cs] = tail


def ffn1_seq(x, wg, wu, cw, cb, seq_len, layer=None):
    M, D = x.shape
    F = wg.shape[-1]
    tm = min(seq_len, 1024)
    tn = FFN_TN
    nj = pl.cdiv(F, tn)
    return pl.pallas_call(
        functools.partial(_ffn1_seq_body, tiles_per_seq=seq_len // tm),
        out_shape=(jax.ShapeDtypeStruct((M, F), BF16), jax.ShapeDtypeStruct((M // tm, SUBLANES, F), F32)),
        grid=(M // tm, nj),
        in_specs=[
            pl.BlockSpec((tm, D), lambda i, j: (i, 0)),
            _w_spec(wg, layer, D, tn, lambda i, j: j),
            _w_spec(wu, layer, D, tn, lambda i, j: j),
            pl.BlockSpec((FFN_CONV, tn), lambda i, j: (0, j)),
            pl.BlockSpec((1, tn), lambda i, j: (0, j)),
        ],
        out_specs=(pl.BlockSpec((tm, tn), lambda i, j: (i, j)), pl.BlockSpec((1, SUBLANES, tn), lambda i, j: (i, 0, j))),
        scratch_shapes=[pltpu.VMEM((nj, SUBLANES, tn), F32), pltpu.VMEM((tn // FFN_TC, tm + SUBLANES, FFN_TC), F32)],
        compiler_params=_cparams("arbitrary", "arbitrary"),
        name="ffn1_seq",
    )(x, wg, wu, cw, cb.reshape(1, F))


def _ffn1_step_body(x_ref, wg_ref, wu_ref, cw_ref, cb_ref, s0_ref, s1_ref, act_ref, g_ref, wgb_ref, wub_ref):
    x = x_ref[...]
    wg = wg_ref[...].astype(BF16)
    wu = wu_ref[...].astype(BF16)
    wgb_ref[...] = wg
    wub_ref[...] = wu
    g = jnp.dot(x, wg, preferred_element_type=F32)
    u = jnp.dot(x, wu, preferred_element_type=F32)
    w = cw_ref[...]
    y = s0_ref[...] * w[0:1] + s1_ref[...] * w[1:2] + g * w[2:3] + cb_ref[...]
    act_ref[...] = (_silu(y) * u).astype(act_ref.dtype)
    g_ref[...] = g


def ffn1_step(x, wg, wu, cw, cb, s0, s1, layer=None):
    M, D = x.shape
    F = wg.shape[-1]
    tn = FFN_TC
    col = lambda j: (0, j)
    return pl.pallas_call(
        _ffn1_step_body,
        out_shape=(jax.ShapeDtypeStruct((M, F), BF16), jax.ShapeDtypeStruct((M, F), F32),
                   jax.ShapeDtypeStruct((D, F), BF16), jax.ShapeDtypeStruct((D, F), BF16)),
        grid=(pl.cdiv(F, tn),),
        in_specs=[
            pl.BlockSpec((M, D), lambda j: (0, 0)),
            _w_spec(wg, layer, D, tn, lambda j: j),
            _w_spec(wu, layer, D, tn, lambda j: j),
            pl.BlockSpec((FFN_CONV, tn), col),
            pl.BlockSpec((1, tn), col),
            pl.BlockSpec((M, tn), col),
            pl.BlockSpec((M, tn), col),
        ],
        out_specs=(pl.BlockSpec((M, tn), col), pl.BlockSpec((M, tn), col),
                   pl.BlockSpec((D, tn), col), pl.BlockSpec((D, tn), col)),
        compiler_params=_cparams("parallel"),
        name="ffn1_step",
    )(x, wg, wu, cw, cb.reshape(1, F), s0, s1)


CONV_TC = 1024


def _conv_seq_body(x_ref, w_ref, b_ref, o_ref, buf_ref, *, width):
    t = pl.program_id(2)
    tt = x_ref.shape[0]

    @pl.when(t == 0)
    def _():
        buf_ref[0:SUBLANES, :] = jnp.zeros((SUBLANES, buf_ref.shape[1]), F32)

    x = x_ref[...]
    buf_ref[SUBLANES:SUBLANES + tt, :] = x
    w = w_ref[...]
    y = x * w[width - 1:width]
    for j in range(1, width):
        y = y + buf_ref[SUBLANES - j:SUBLANES - j + tt, :] * w[width - 1 - j:width - j]
    o_ref[...] = _silu(y + b_ref[...])
    buf_ref[0:SUBLANES, :] = x[tt - SUBLANES:tt, :]


def conv_seq(proj, col0, w, b, nseq, seq_len):
    width, C = w.shape
    tc = CONV_TC
    tt = min(seq_len, 512)
    nt = seq_len // tt
    c0 = col0 // tc
    return pl.pallas_call(
        functools.partial(_conv_seq_body, width=width),
        out_shape=jax.ShapeDtypeStruct((nseq * seq_len, C), F32),
        grid=(nseq, C // tc, nt),
        in_specs=[
            pl.BlockSpec((tt, tc), lambda s, c, t: (s * nt + t, c0 + c)),
            pl.BlockSpec((width, tc), lambda s, c, t: (0, c)),
            pl.BlockSpec((1, tc), lambda s, c, t: (0, c)),
        ],
        out_specs=pl.BlockSpec((tt, tc), lambda s, c, t: (s * nt + t, c)),
        scratch_shapes=[pltpu.VMEM((tt + SUBLANES, tc), F32)],
        compiler_params=_cparams("parallel", "parallel", "arbitrary"),
        name="conv_seq",
    )(proj, w, b.reshape(1, C))


def _conv_step_body(x_ref, s0_ref, s1_ref, s2_ref, w_ref, b_ref, o_ref):
    w = w_ref[...]
    y = s0_ref[...] * w[0:1] + s1_ref[...] * w[1:2] + s2_ref[...] * w[2:3] + x_ref[...] * w[3:4] + b_ref[...]
    o_ref[...] = _silu(y)


def conv_step(proj, col0, w, b, hist):
    width, C = w.shape
    M = proj.shape[0]
    tc = CONV_TC
    c0 = col0 // tc
    spec = pl.BlockSpec((M, tc), lambda c: (0, c))
    return pl.pallas_call(
        _conv_step_body,
        out_shape=jax.ShapeDtypeStruct((M, C), F32),
        grid=(C // tc,),
        in_specs=[pl.BlockSpec((M, tc), lambda c: (0, c0 + c)), spec, spec, spec,
                  pl.BlockSpec((width, tc), lambda c: (0, c)), pl.BlockSpec((1, tc), lambda c: (0, c))],
        out_specs=spec,
        compiler_params=_cparams("parallel"),
        name="conv_step",
    )(proj, hist[:, 0], hist[:, 1], hist[:, 2], w, b.reshape(1, C))


def _ret_log_gamma(h):
    return jnp.log(1.0 - jnp.exp2(-5.0 - jnp.full((1, 1), h, jnp.int32).astype(F32)))


def _rope(x, cos, sin):
    half = x.shape[-1] // 2
    x1, x2 = x[:, :half], x[:, half:]
    return jnp.concatenate([x1 * cos - x2 * sin, x1 * sin + x2 * cos], axis=-1)


RET_H_STEP = 4


def _ret_chunk_body(q_ref, k_ref, v_ref, g_ref, cos_ref, sin_ref, o_ref, s_ref):
    hg = pl.program_id(1)
    c = pl.program_id(2)
    L = q_ref.shape[0]
    heads = range(RET_H_STEP)

    @pl.when(c == 0)
    def _():
        s_ref[...] = jnp.zeros(s_ref.shape, F32)

    ri, ci = _tri_masks(L)
    rel = (ri - ci).astype(F32)
    idx = lax.broadcasted_iota(jnp.int32, (L, 1), 0).astype(F32)
    cos, sin = cos_ref[...], sin_ref[...]
    lg = [_ret_log_gamma(hg * RET_H_STEP + e) for e in heads]
    q = [_rope(q_ref[:, e * RET_DK:(e + 1) * RET_DK], cos, sin) for e in heads]
    k = [_rope(k_ref[:, e * RET_DK:(e + 1) * RET_DK], cos, sin) * (RET_DK ** -0.5) for e in heads]
    v = [v_ref[:, e * RET_DV:(e + 1) * RET_DV] for e in heads]
    s = [s_ref[0, e] for e in heads]
    sc = [_bdot_nt(q[e], k[e]) * jnp.where(rel >= 0, jnp.exp(lg[e] * jnp.maximum(rel, 0.0)), 0.0) for e in heads]
    o = [_bdot(sc[e], v[e]) + _bdot(q[e], s[e]) * jnp.exp((idx + 1.0) * lg[e]) for e in heads]
    s_new = [jnp.exp(L * lg[e]) * s[e] + _bdot_tn(k[e] * jnp.exp((L - 1.0 - idx) * lg[e]), v[e]) for e in heads]
    for e in heads:
        s_ref[0, e] = s_new[e]
        o_ref[:, e * RET_DV:(e + 1) * RET_DV] = (
            _silu(g_ref[:, e * RET_DV:(e + 1) * RET_DV]) * _rms(o[e])).astype(o_ref.dtype)


def _rope_tables(pos):
    half = RET_DK // 2
    inv_freq = 1.0 / (ROPE_BASE ** jnp.linspace(0.0, 1.0, half, dtype=F32))
    ang = pos.astype(F32)[:, None] * inv_freq[None, :]
    return jnp.cos(ang), jnp.sin(ang)


def ret_prompt(proj, nseq, seq_len):
    L = min(CHUNK, seq_len)
    nc = seq_len // L
    M = nseq * seq_len
    cos, sin = _rope_tables(jnp.arange(seq_len))
    hs = RET_H_STEP
    KW, VW = hs * RET_DK, hs * RET_DV
    kq, kv, kg = RET_QK_DIM // KW, (2 * RET_QK_DIM) // VW, (2 * RET_QK_DIM + RET_V_DIM) // VW
    return pl.pallas_call(
        _ret_chunk_body,
        out_shape=(jax.ShapeDtypeStruct((M, RET_V_DIM), BF16),
                   jax.ShapeDtypeStruct((nseq, RET_HEADS, RET_DK, RET_DV), F32)),
        grid=(nseq, RET_HEADS // hs, nc),
        in_specs=[
            pl.BlockSpec((L, KW), lambda b, h, c: (b * nc + c, h)),
            pl.BlockSpec((L, KW), lambda b, h, c: (b * nc + c, kq + h)),
            pl.BlockSpec((L, VW), lambda b, h, c: (b * nc + c, kv + h)),
            pl.BlockSpec((L, VW), lambda b, h, c: (b * nc + c, kg + h)),
            pl.BlockSpec((L, RET_DK // 2), lambda b, h, c: (c, 0)),
            pl.BlockSpec((L, RET_DK // 2), lambda b, h, c: (c, 0)),
        ],
        out_specs=(pl.BlockSpec((L, VW), lambda b, h, c: (b * nc + c, h)),
                   pl.BlockSpec((1, hs, RET_DK, RET_DV), lambda b, h, c: (b, h, 0, 0))),
        compiler_params=_cparams("parallel", "parallel", "arbitrary"),
        name="ret_chunk",
    )(proj, proj, proj, proj, cos, sin)


STEP_TB = 8


def _ret_step_body(q_ref, k_ref, v_ref, g_ref, cos_ref, sin_ref, s_ref, o_ref, so_ref):
    h = pl.program_id(1)
    tb = q_ref.shape[0]
    gamma = jnp.exp(_ret_log_gamma(h))
    cos, sin = cos_ref[...], sin_ref[...]
    q = _rope(q_ref[...], cos, sin)
    k = _rope(k_ref[...], cos, sin) * (RET_DK ** -0.5)
    v = v_ref[...]
    g = g_ref[...]
    sc = jnp.sum(q * k, axis=-1, keepdims=True)
    qt = q.T
    kt = k.T
    for b in range(tb):
        s = s_ref[b, 0]
        vb = v[b:b + 1, :]
        qs = jnp.sum(s * qt[:, b:b + 1], axis=0, keepdims=True)
        o = sc[b:b + 1, :] * vb + qs * gamma
        so_ref[b, 0] = gamma * s + kt[:, b:b + 1] * vb
        o_ref[b:b + 1, :] = (_silu(g[b:b + 1, :]) * _rms(o)).astype(o_ref.dtype)


def ret_step(proj, state, pos):
    M = proj.shape[0]
    tb = STEP_TB
    cos, sin = _rope_tables(jnp.full((1,), pos))
    kq, kv, kg = RET_QK_DIM // RET_DK, (2 * RET_QK_DIM) // RET_DV, (2 * RET_QK_DIM + RET_V_DIM) // RET_DV
    s_spec = pl.BlockSpec((tb, 1, RET_DK, RET_DV), lambda i, h: (i, h, 0, 0))
    return pl.pallas_call(
        _ret_step_body,
        out_shape=(jax.ShapeDtypeStruct((M, RET_V_DIM), BF16), jax.ShapeDtypeStruct(state.shape, F32)),
        grid=(M // tb, RET_HEADS),
        in_specs=[
            pl.BlockSpec((tb, RET_DK), lambda i, h: (i, h)),
            pl.BlockSpec((tb, RET_DK), lambda i, h: (i, kq + h)),
            pl.BlockSpec((tb, RET_DV), lambda i, h: (i, kv + h)),
            pl.BlockSpec((tb, RET_DV), lambda i, h: (i, kg + h)),
            pl.BlockSpec((1, RET_DK // 2), lambda i, h: (0, 0)),
            pl.BlockSpec((1, RET_DK // 2), lambda i, h: (0, 0)),
            s_spec,
        ],
        out_specs=(pl.BlockSpec((tb, RET_DV), lambda i, h: (i, h)), s_spec),
        compiler_params=_cparams("parallel", "parallel"),
        name="ret_step",
    )(proj, proj, proj, proj, cos, sin, state)


def _ssm_chunk_body(x_ref, b_ref, c_ref, z_ref, dtr_ref, dtc_ref, pr_ref, pc_ref, dskip_ref, nw_ref, y_ref, s_ref):
    c = pl.program_id(2)
    L = x_ref.shape[0]
    P2 = 2 * SSM_HEAD_DIM

    @pl.when(c == 0)
    def _():
        s_ref[...] = jnp.zeros(s_ref.shape, F32)

    pr = pr_ref[0]
    pc = pc_ref[0]
    dt_r = _softplus(dtr_ref[0] + pr[0:1, :])
    dt_c = _softplus(dtc_ref[0] + pc[:, 0:1])
    acs_r = _cumsum_rows(dt_r * (-jnp.exp(pr[1:2, :])))
    acs_c = _cumsum_cols(dt_c * (-jnp.exp(pc[:, 1:2])))
    bm = b_ref[...]
    cm = c_ref[...]
    x = x_ref[...]
    cb = _bdot_nt(cm, bm)
    ri, ci = _tri_masks(L)
    incl = ri >= ci
    lo_lane = lax.broadcasted_iota(jnp.int32, (L, P2), 1) < SSM_HEAD_DIM
    lo_row = lax.broadcasted_iota(jnp.int32, (P2, P2), 0) < SSM_HEAD_DIM
    a_end = acs_r[L - 1:L, :]
    acs_b = _lane_bcast_cols(acs_r, LANES)
    ea2 = _lane_bcast_cols(jnp.exp(acs_r), SSM_HEAD_DIM)
    we2 = _lane_bcast_cols(jnp.exp(a_end - acs_r) * dt_r, SSM_HEAD_DIM)
    e_end = jnp.broadcast_to(jnp.exp(acs_c[:, L - 1:L]), (SSM_HPG, P2))
    pairs = range(SSM_HPG // 2)
    sc = [[cb * jnp.exp(jnp.where(incl, acs_b[:, h * LANES:(h + 1) * LANES] - acs_c[h:h + 1, :], -jnp.inf))
           * dt_c[h:h + 1, :] for h in (2 * p, 2 * p + 1)] for p in pairs]
    xp = [x[:, p * P2:(p + 1) * P2] for p in pairs]
    x2 = [jnp.concatenate([jnp.where(lo_lane, xp[p], 0.0), jnp.where(lo_lane, 0.0, xp[p])], axis=0) for p in pairs]
    st = [s_ref[0, 2 * p:2 * p + 2].reshape(P2, SSM_STATE) for p in pairs]
    ys = [_bdot(jnp.concatenate(sc[p], axis=1), x2[p]) + _bdot_nt(cm, st[p]) * ea2[:, p * P2:(p + 1) * P2]
          for p in pairs]
    st_new = [jnp.where(lo_row, e_end[2 * p:2 * p + 1, :], e_end[2 * p + 1:2 * p + 2, :]) * st[p]
              + _bdot_tn(xp[p] * we2[:, p * P2:(p + 1) * P2], bm) for p in pairs]
    for p in pairs:
        s_ref[0, 2 * p:2 * p + 2] = st_new[p].reshape(2, SSM_HEAD_DIM, SSM_STATE)
    y = jnp.concatenate(ys, axis=1) + dskip_ref[...] * x
    y = y * _silu(z_ref[...])
    y_ref[...] = (_rms(y) * nw_ref[...]).astype(y_ref.dtype)


def _ssm_params(dt_bias, a_log):
    pr = jnp.stack([dt_bias.reshape(SSM_GROUPS, SSM_HPG), a_log.reshape(SSM_GROUPS, SSM_HPG)], axis=1)
    return pr, jnp.swapaxes(pr, 1, 2)


def ssm_prompt(proj, xa, dt_bias, a_log, d_skip, norm_w, nseq, seq_len):
    L = min(CHUNK, seq_len)
    nc = seq_len // L
    M = nseq * seq_len
    W = SSM_GROUP_W
    dt_raw = proj[:, SSM_D_INNER + SSM_CONV_DIM:]
    dtr = jnp.transpose(dt_raw.reshape(M, SSM_GROUPS, SSM_HPG), (1, 0, 2))
    dtc = jnp.swapaxes(dtr, 1, 2)
    pr, pc = _ssm_params(dt_bias, a_log)
    dskip = jnp.repeat(d_skip, SSM_HEAD_DIM).reshape(1, SSM_D_INNER)
    kb = SSM_D_INNER // SSM_STATE
    return pl.pallas_call(
        _ssm_chunk_body,
        out_shape=(jax.ShapeDtypeStruct((M, SSM_D_INNER), BF16),
                   jax.ShapeDtypeStruct((nseq, SSM_HEADS, SSM_HEAD_DIM, SSM_STATE), F32)),
        grid=(nseq, SSM_GROUPS, nc),
        in_specs=[
            pl.BlockSpec((L, W), lambda b, g, c: (b * nc + c, g)),
            pl.BlockSpec((L, SSM_STATE), lambda b, g, c: (b * nc + c, kb + g)),
            pl.BlockSpec((L, SSM_STATE), lambda b, g, c: (b * nc + c, kb + SSM_GROUPS + g)),
            pl.BlockSpec((L, W), lambda b, g, c: (b * nc + c, g)),
            pl.BlockSpec((1, L, SSM_HPG), lambda b, g, c: (g, b * nc + c, 0)),
            pl.BlockSpec((1, SSM_HPG, L), lambda b, g, c: (g, 0, b * nc + c)),
            pl.BlockSpec((1, 2, SSM_HPG), lambda b, g, c: (g, 0, 0)),
            pl.BlockSpec((1, SSM_HPG, 2), lambda b, g, c: (g, 0, 0)),
            pl.BlockSpec((1, W), lambda b, g, c: (0, g)),
            pl.BlockSpec((1, W), lambda b, g, c: (0, g)),
        ],
        out_specs=(pl.BlockSpec((L, W), lambda b, g, c: (b * nc + c, g)),
                   pl.BlockSpec((1, SSM_HPG, SSM_HEAD_DIM, SSM_STATE), lambda b, g, c: (b, g, 0, 0))),
        compiler_params=_cparams("parallel", "parallel", "arbitrary"),
        name="ssm_chunk",
    )(xa, xa, xa, proj, dtr, dtc, pr, pc, dskip, norm_w.reshape(1, SSM_D_INNER))


def _ssm_step_body(x_ref, b_ref, c_ref, z_ref, dtx_ref, bias_ref, alog_ref, dskip_ref, nw_ref, s_ref, y_ref, so_ref):
    tb = x_ref.shape[0]
    W = SSM_GROUP_W
    x = x_ref[...]
    bm = b_ref[...]
    cm = c_ref[...]
    dt = _softplus(dtx_ref[...] + bias_ref[...])
    da = jnp.exp(dt * (-jnp.exp(alog_ref[...])))
    cb = jnp.sum(cm * bm, axis=-1, keepdims=True)
    xdt = (x * dt).T
    dat = da.T
    rows = lax.broadcasted_iota(jnp.int32, (tb, W), 0)
    cs = jnp.zeros((tb, W), F32)
    for b in range(tb):
        st = s_ref[b].reshape(W, SSM_STATE)
        cs = jnp.where(rows == b, _bdot_nt(cm, st), cs)
        st_new = dat[:, b:b + 1] * st + xdt[:, b:b + 1] * bm[b:b + 1, :]
        so_ref[b] = st_new.reshape(SSM_HPG, SSM_HEAD_DIM, SSM_STATE)
    y = cb * dt * x + cs * da + dskip_ref[...] * x
    y = y * _silu(z_ref[...])
    y_ref[...] = (_rms(y) * nw_ref[...]).astype(y_ref.dtype)


def ssm_step(proj, xa, state, dt_bias, a_log, d_skip, norm_w):
    M = proj.shape[0]
    tb = STEP_TB
    W = SSM_GROUP_W
    rep = lambda a: jnp.repeat(a, SSM_HEAD_DIM, axis=-1)
    dtx = rep(proj[:, SSM_D_INNER + SSM_CONV_DIM:])
    kb = SSM_D_INNER // SSM_STATE
    s_spec = pl.BlockSpec((tb, SSM_HPG, SSM_HEAD_DIM, SSM_STATE), lambda g, i: (i, g, 0, 0))
    row = pl.BlockSpec((1, W), lambda g, i: (0, g))
    tile = pl.BlockSpec((tb, W), lambda g, i: (i, g))
    return pl.pallas_call(
        _ssm_step_body,
        out_shape=(jax.ShapeDtypeStruct((M, SSM_D_INNER), BF16), jax.ShapeDtypeStruct(state.shape, F32)),
        grid=(SSM_GROUPS, M // tb),
        in_specs=[
            tile,
            pl.BlockSpec((tb, SSM_STATE), lambda g, i: (i, kb + g)),
            pl.BlockSpec((tb, SSM_STATE), lambda g, i: (i, kb + SSM_GROUPS + g)),
            tile, tile, row, row, row, row, s_spec,
        ],
        out_specs=(tile, s_spec),
        compiler_params=_cparams("parallel", "parallel"),
        name="ssm_step",
    )(xa, xa, xa, proj, dtx, rep(dt_bias).reshape(1, -1), rep(a_log).reshape(1, -1), rep(d_skip).reshape(1, -1),
      norm_w.reshape(1, -1), state)


def _l2n(x):
    return x * lax.rsqrt(jnp.sum(x * x, axis=-1, keepdims=True) + NORM_EPS)


def _unit_lower_inverses(mats, ri, ci):
    L = mats[0].shape[0]
    eye = (ri == ci).astype(F32)
    same8 = (ri >> 3) == (ci >> 3)
    ns = [jnp.where(same8, -a, 0.0) for a in mats]
    n2 = [_bdot(n, n) for n in ns]
    n4 = [_bdot(x, x) for x in n2]
    ts = [eye + n for n in ns]
    ts = [t + _bdot(t, x) for t, x in zip(ts, n2)]
    ts = [t + _bdot(t, x) for t, x in zip(ts, n4)]
    sh = 3
    while (1 << sh) < L:
        lower_left = ((ri >> (sh + 1)) == (ci >> (sh + 1))) & ((ri >> sh) != (ci >> sh))
        us = [_bdot(jnp.where(lower_left, a, 0.0), t) for a, t in zip(mats, ts)]
        ts = [t - _bdot(t, u) for t, u in zip(ts, us)]
        sh += 1
    return ts


def _gdn_chunk_body(q_ref, k_ref, v_ref, z_ref, gr_ref, gc_ref, pr_ref, pc_ref, nw_ref, o_ref, s_ref, *, ks):
    c = pl.program_id(2)
    L = q_ref.shape[0]
    vs = ks * GDN_REP

    @pl.when(c == 0)
    def _():
        s_ref[...] = jnp.zeros(s_ref.shape, F32)

    gr = gr_ref[0]
    gc = gc_ref[0]
    pr = pr_ref[0]
    pc = pc_ref[0]
    beta = jax.nn.sigmoid(gr[:, 0:vs])
    gt_r = _cumsum_rows(-jnp.exp(pr[0:1, :]) * _softplus(gr[:, vs:] + pr[1:2, :]))
    gt_c = _cumsum_cols(-jnp.exp(pc[:, 0:1]) * _softplus(gc[vs:, :] + pc[:, 1:2]))
    ri, ci = _tri_masks(L)
    nw = nw_ref[...]
    qs, ks_ = [], []
    for kh in range(ks):
        qs.append(_l2n(q_ref[:, kh * GDN_DK:(kh + 1) * GDN_DK]) * (GDN_DK ** -0.5))
        ks_.append(_l2n(k_ref[:, kh * GDN_DK:(kh + 1) * GDN_DK]))
    kks = [_bdot_nt(k, k) for k in ks_]
    qks = [_bdot_nt(q, k) for q, k in zip(qs, ks_)]
    heads = range(vs)
    kof = [j // GDN_REP for j in heads]
    st = [s_ref[0, j] for j in heads]
    gcol = [gt_r[:, j:j + 1] for j in heads]
    seg = [gcol[j] - gt_c[j:j + 1, :] for j in heads]
    bcol = [beta[:, j:j + 1] for j in heads]
    eg = [jnp.exp(gcol[j]) for j in heads]
    a_mats = [kks[kof[j]] * jnp.exp(jnp.where(ri > ci, seg[j], -jnp.inf)) * bcol[j] for j in heads]
    rhs = [bcol[j] * (v_ref[:, j * GDN_DV:(j + 1) * GDN_DV] - eg[j] * _bdot(ks_[kof[j]], st[j])) for j in heads]
    tinv = _unit_lower_inverses(a_mats, ri, ci)
    delta = [_bdot(tinv[j], rhs[j]) for j in heads]
    o = [eg[j] * _bdot(qs[kof[j]], st[j])
         + _bdot(qks[kof[j]] * jnp.exp(jnp.where(ri >= ci, seg[j], -jnp.inf)), delta[j]) for j in heads]
    g_end = [gt_r[L - 1:L, j:j + 1] for j in heads]
    s_new = [jnp.exp(g_end[j]) * st[j] + _bdot_tn(ks_[kof[j]] * jnp.exp(g_end[j] - gcol[j]), delta[j]) for j in heads]
    for j in heads:
        s_ref[0, j] = s_new[j]
        ze = z_ref[:, j * GDN_DV:(j + 1) * GDN_DV]
        o_ref[:, j * GDN_DV:(j + 1) * GDN_DV] = (_rms(o[j]) * nw * _silu(ze)).astype(o_ref.dtype)


def _gdn_gate_inputs(proj, ks):
    M = proj.shape[0]
    ng, vs = GDN_K_HEADS // ks, ks * GDN_REP
    tail = proj[:, GDN_CONV_DIM + GDN_VAL_DIM:]
    braw = tail[:, :GDN_V_HEADS].reshape(M, ng, vs)
    araw = tail[:, GDN_V_HEADS:].reshape(M, ng, vs)
    gr = jnp.transpose(jnp.concatenate([braw, araw], axis=-1), (1, 0, 2))
    return gr, jnp.swapaxes(gr, 1, 2)


def _gdn_params(a_log, dt_bias, ks):
    ng, vs = GDN_K_HEADS // ks, ks * GDN_REP
    pr = jnp.stack([a_log.reshape(ng, vs), dt_bias.reshape(ng, vs)], axis=1)
    return pr, jnp.swapaxes(pr, 1, 2)


GDN_KH_STEP = 8


def gdn_prompt(proj, qkv, a_log, dt_bias, norm_w, nseq, seq_len):
    L = min(CHUNK, seq_len)
    nc = seq_len // L
    M = nseq * seq_len
    ks = GDN_KH_STEP
    ng, vs = GDN_K_HEADS // ks, ks * GDN_REP
    KW, VW = ks * GDN_DK, vs * GDN_DV
    gr, gc = _gdn_gate_inputs(proj, ks)
    pr, pc = _gdn_params(a_log, dt_bias, ks)
    kk, kv, kz = GDN_KEY_DIM // KW, (2 * GDN_KEY_DIM) // VW, GDN_CONV_DIM // VW
    return pl.pallas_call(
        functools.partial(_gdn_chunk_body, ks=ks),
        out_shape=(jax.ShapeDtypeStruct((M, GDN_VAL_DIM), BF16),
                   jax.ShapeDtypeStruct((nseq, GDN_V_HEADS, GDN_DK, GDN_DV), F32)),
        grid=(nseq, ng, nc),
        in_specs=[
            pl.BlockSpec((L, KW), lambda b, h, c: (b * nc + c, h)),
            pl.BlockSpec((L, KW), lambda b, h, c: (b * nc + c, kk + h)),
            pl.BlockSpec((L, VW), lambda b, h, c: (b * nc + c, kv + h)),
            pl.BlockSpec((L, VW), lambda b, h, c: (b * nc + c, kz + h)),
            pl.BlockSpec((1, L, 2 * vs), lambda b, h, c: (h, b * nc + c, 0)),
            pl.BlockSpec((1, 2 * vs, L), lambda b, h, c: (h, 0, b * nc + c)),
            pl.BlockSpec((1, 2, vs), lambda b, h, c: (h, 0, 0)),
            pl.BlockSpec((1, vs, 2), lambda b, h, c: (h, 0, 0)),
            pl.BlockSpec((1, GDN_DV), lambda b, h, c: (0, 0)),
        ],
        out_specs=(pl.BlockSpec((L, VW), lambda b, h, c: (b * nc + c, h)),
                   pl.BlockSpec((1, vs, GDN_DK, GDN_DV), lambda b, h, c: (b, h, 0, 0))),
        compiler_params=_cparams("parallel", "parallel", "arbitrary"),
        name="gdn_chunk",
    )(qkv, qkv, qkv, proj, gr, gc, pr, pc, norm_w.reshape(1, GDN_DV))


def _gdn_step_body(q_ref, k_ref, v_ref, z_ref, gr_ref, pr_ref, nw_ref, s_ref, o_ref, so_ref, *, ks):
    tb = q_ref.shape[0]
    vs = ks * GDN_REP
    gr = gr_ref[0]
    pr = pr_ref[0]
    beta = jax.nn.sigmoid(gr[:, 0:vs])
    eg_all = jnp.exp(-jnp.exp(pr[0:1, :]) * _softplus(gr[:, vs:] + pr[1:2, :]))
    nw = nw_ref[...]
    for kh in range(ks):
        q = _l2n(q_ref[:, kh * GDN_DK:(kh + 1) * GDN_DK]) * (GDN_DK ** -0.5)
        k = _l2n(k_ref[:, kh * GDN_DK:(kh + 1) * GDN_DK])
        qk = jnp.sum(q * k, axis=-1, keepdims=True)
        qb = _lane_bcast_cols(q.T)
        kb = _lane_bcast_cols(k.T)
        for b in range(tb):
            kcol = kb[:, b * LANES:(b + 1) * LANES]
            qcol = qb[:, b * LANES:(b + 1) * LANES]
            for e in range(GDN_REP):
                j = kh * GDN_REP + e
                s = s_ref[b, j]
                eg = eg_all[b:b + 1, j:j + 1]
                k_s = jnp.sum(s * kcol, axis=0, keepdims=True)
                q_s = jnp.sum(s * qcol, axis=0, keepdims=True)
                delta = beta[b:b + 1, j:j + 1] * (v_ref[b:b + 1, j * GDN_DV:(j + 1) * GDN_DV] - eg * k_s)
                o = eg * q_s + qk[b:b + 1, :] * delta
                so_ref[b, j] = eg * s + kcol * delta
                ze = z_ref[b:b + 1, j * GDN_DV:(j + 1) * GDN_DV]
                o_ref[b:b + 1, j * GDN_DV:(j + 1) * GDN_DV] = (_rms(o) * nw * _silu(ze)).astype(o_ref.dtype)


GDN_KH_STEP_DECODE = 4


def gdn_step(proj, qkv, state, a_log, dt_bias, norm_w):
    M = proj.shape[0]
    tb = STEP_TB
    ks = GDN_KH_STEP_DECODE
    ng, vs = GDN_K_HEADS // ks, ks * GDN_REP
    KW, VW = ks * GDN_DK, vs * GDN_DV
    gr, _ = _gdn_gate_inputs(proj, ks)
    pr, _ = _gdn_params(a_log, dt_bias, ks)
    kk, kv, kz = GDN_KEY_DIM // KW, (2 * GDN_KEY_DIM) // VW, GDN_CONV_DIM // VW
    s_spec = pl.BlockSpec((tb, vs, GDN_DK, GDN_DV), lambda i, h: (i, h, 0, 0))
    return pl.pallas_call(
        functools.partial(_gdn_step_body, ks=ks),
        out_shape=(jax.ShapeDtypeStruct((M, GDN_VAL_DIM), BF16), jax.ShapeDtypeStruct(state.shape, F32)),
        grid=(M // tb, ng),
        in_specs=[
            pl.BlockSpec((tb, KW), lambda i, h: (i, h)),
            pl.BlockSpec((tb, KW), lambda i, h: (i, kk + h)),
            pl.BlockSpec((tb, VW), lambda i, h: (i, kv + h)),
            pl.BlockSpec((tb, VW), lambda i, h: (i, kz + h)),
            pl.BlockSpec((1, tb, 2 * vs), lambda i, h: (h, i, 0)),
            pl.BlockSpec((1, 2, vs), lambda i, h: (h, 0, 0)),
            pl.BlockSpec((1, GDN_DV), lambda i, h: (0, 0)),
            s_spec,
        ],
        out_specs=(pl.BlockSpec((tb, VW), lambda i, h: (i, h)), s_spec),
        compiler_params=_cparams("parallel", "parallel"),
        name="gdn_step",
    )(qkv, qkv, qkv, proj, gr, pr, norm_w.reshape(1, GDN_DV), state)


def _log_sigmoid(x):
    return -_softplus(-x)


ML_H_STEP = 4


def _ml_chunk_body(q_ref, k_ref, v_ref, og_ref, gr_ref, gc_ref, pr_ref, pc_ref, nw_ref, o_ref, c_ref, n_ref, m_ref):
    ch = pl.program_id(2)
    L = q_ref.shape[0]

    @pl.when(ch == 0)
    def _():
        c_ref[...] = jnp.zeros(c_ref.shape, F32)
        n_ref[...] = jnp.zeros(n_ref.shape, F32)
        m_ref[...] = jnp.zeros(m_ref.shape, F32)

    ri, ci = _tri_masks(L)
    H = range(ML_H_STEP)
    q = [q_ref[:, e * ML_DQK:(e + 1) * ML_DQK] for e in H]
    k = [k_ref[:, e * ML_DQK:(e + 1) * ML_DQK] * (ML_DQK ** -0.5) for e in H]
    v = [v_ref[:, e * ML_DV:(e + 1) * ML_DV] for e in H]
    gr = [gr_ref[e] + pr_ref[e] for e in H]
    gc = [gc_ref[e] + pc_ref[e] for e in H]
    it_c = [gr[e][:, 0:1] for e in H]
    it_r = [gc[e][0:1, :] for e in H]
    ft_c = [_cumsum_rows(_log_sigmoid(gr[e][:, 1:2])) for e in H]
    ft_r = [_cumsum_cols(_log_sigmoid(gc[e][1:2, :])) for e in H]
    c_st = [c_ref[0, e] for e in H]
    n_st = [n_ref[0, e] for e in H]
    m_st = [m_ref[0, e][:, 0:1] for e in H]
    dmat = [jnp.where(ri >= ci, ft_c[e] - ft_r[e] + it_r[e], -jnp.inf) for e in H]
    inter = [ft_c[e] + m_st[e] for e in H]
    mi = [jnp.maximum(inter[e], jnp.max(dmat[e], axis=-1, keepdims=True)) for e in H]
    winter = [jnp.exp(inter[e] - mi[e]) for e in H]
    sc = [_bdot_nt(q[e], k[e]) * jnp.exp(dmat[e] - mi[e]) for e in H]
    num = [winter[e] * _bdot(q[e], c_st[e]) + _bdot(sc[e], v[e]) for e in H]
    den = [winter[e] * jnp.sum(q[e] * n_st[e], axis=-1, keepdims=True) + jnp.sum(sc[e], axis=-1, keepdims=True)
           for e in H]
    hh = [num[e] / jnp.maximum(jnp.abs(den[e]), jnp.exp(-mi[e])) for e in H]
    m_new = [mi[e][L - 1:L, :] for e in H]
    f_end = [ft_c[e][L - 1:L, :] for e in H]
    keep = [jnp.exp(f_end[e] + m_st[e] - m_new[e]) for e in H]
    kw = [k[e] * jnp.exp(f_end[e] - ft_c[e] + it_c[e] - m_new[e]) for e in H]
    c_new = [keep[e] * c_st[e] + _bdot_tn(kw[e], v[e]) for e in H]
    for e in H:
        c_ref[0, e] = c_new[e]
        n_ref[0, e] = keep[e] * n_st[e] + jnp.sum(kw[e], axis=0, keepdims=True)
        m_ref[0, e] = jnp.broadcast_to(m_new[e], (1, LANES))
        vs = slice(e * ML_DV, (e + 1) * ML_DV)
        o_ref[:, vs] = (_rms(hh[e]) * nw_ref[:, vs] * jax.nn.sigmoid(og_ref[:, vs])).astype(o_ref.dtype)


def _ml_gate_inputs(proj, gate_b):
    M = proj.shape[0]
    tail = proj[:, 2 * ML_QK_DIM + 2 * ML_V_DIM:]
    gr = jnp.transpose(tail.reshape(M, 2, ML_HEADS), (2, 0, 1))
    pr = jnp.transpose(gate_b.reshape(2, ML_HEADS), (1, 0)).reshape(ML_HEADS, 1, 2)
    return gr, jnp.swapaxes(gr, 1, 2), pr, jnp.swapaxes(pr, 1, 2)


def ml_prompt(proj, gate_b, norm_w, nseq, seq_len):
    L = min(CHUNK, seq_len)
    nc = seq_len // L
    M = nseq * seq_len
    gr, gc, pr, pc = _ml_gate_inputs(proj, gate_b)
    hs = ML_H_STEP
    KW, VW = hs * ML_DQK, hs * ML_DV
    kk, kv, ko = ML_QK_DIM // KW, (2 * ML_QK_DIM) // VW, (2 * ML_QK_DIM + ML_V_DIM) // VW
    return pl.pallas_call(
        _ml_chunk_body,
        out_shape=(jax.ShapeDtypeStruct((M, ML_V_DIM), BF16),
                   jax.ShapeDtypeStruct((nseq, ML_HEADS, ML_DQK, ML_DV), F32),
                   jax.ShapeDtypeStruct((nseq, ML_HEADS, 1, ML_DQK), F32),
                   jax.ShapeDtypeStruct((nseq, ML_HEADS, 1, LANES), F32)),
        grid=(nseq, ML_HEADS // hs, nc),
        in_specs=[
            pl.BlockSpec((L, KW), lambda b, h, c: (b * nc + c, h)),
            pl.BlockSpec((L, KW), lambda b, h, c: (b * nc + c, kk + h)),
            pl.BlockSpec((L, VW), lambda b, h, c: (b * nc + c, kv + h)),
            pl.BlockSpec((L, VW), lambda b, h, c: (b * nc + c, ko + h)),
            pl.BlockSpec((hs, L, 2), lambda b, h, c: (h, b * nc + c, 0)),
            pl.BlockSpec((hs, 2, L), lambda b, h, c: (h, 0, b * nc + c)),
            pl.BlockSpec((hs, 1, 2), lambda b, h, c: (h, 0, 0)),
            pl.BlockSpec((hs, 2, 1), lambda b, h, c: (h, 0, 0)),
            pl.BlockSpec((1, VW), lambda b, h, c: (0, h)),
        ],
        out_specs=(pl.BlockSpec((L, VW), lambda b, h, c: (b * nc + c, h)),
                   pl.BlockSpec((1, hs, ML_DQK, ML_DV), lambda b, h, c: (b, h, 0, 0)),
                   pl.BlockSpec((1, hs, 1, ML_DQK), lambda b, h, c: (b, h, 0, 0)),
                   pl.BlockSpec((1, hs, 1, LANES), lambda b, h, c: (b, h, 0, 0))),
        compiler_params=_cparams("parallel", "parallel", "arbitrary"),
        name="ml_chunk",
    )(proj, proj, proj, proj, gr, gc, pr, pc, norm_w.reshape(1, ML_V_DIM))


def _ml_step_body(q_ref, k_ref, v_ref, og_ref, gr_ref, pr_ref, nw_ref, c_ref, n_ref, m_ref, o_ref, co_ref, no_ref, mo_ref):
    tb = q_ref.shape[0]
    q = q_ref[...]
    k = k_ref[...] * (ML_DQK ** -0.5)
    v = v_ref[...]
    og = og_ref[...]
    gr = gr_ref[0] + pr_ref[0]
    it = gr[:, 0:1]
    ft = _log_sigmoid(gr[:, 1:2])
    qk = jnp.sum(q * k, axis=-1, keepdims=True)
    qt = _lane_bcast_cols(q.T)
    kt = _lane_bcast_cols(k.T)
    nw = nw_ref[...]
    reps = ML_DV // LANES
    for b in range(tb):
        qcol = jnp.concatenate([qt[:, b * LANES:(b + 1) * LANES]] * reps, axis=1)
        kcol = jnp.concatenate([kt[:, b * LANES:(b + 1) * LANES]] * reps, axis=1)
        c_st = c_ref[b, 0]
        n_st = n_ref[b, 0]
        m_st = m_ref[b, 0][:, 0:1]
        qb = q[b:b + 1, :]
        kb = k[b:b + 1, :]
        vb = v[b:b + 1, :]
        itb = it[b:b + 1, :]
        ftb = ft[b:b + 1, :]
        inter = ftb + m_st
        mi = jnp.maximum(inter, itb)
        wmat = jnp.exp(itb - mi)
        winter = jnp.exp(inter - mi)
        sc = qk[b:b + 1, :] * wmat
        num = winter * jnp.sum(c_st * qcol, axis=0, keepdims=True) + sc * vb
        den = winter * jnp.sum(qb * n_st, axis=-1, keepdims=True) + sc
        hh = num / jnp.maximum(jnp.abs(den), jnp.exp(-mi))
        wk = jnp.exp(itb - mi)
        keep = jnp.exp(ftb + m_st - mi)
        co_ref[b, 0] = keep * c_st + (kcol * wk) * vb
        no_ref[b, 0] = keep * n_st + kb * wk
        mo_ref[b, 0] = jnp.broadcast_to(mi, (1, LANES))
        o_ref[b:b + 1, :] = (_rms(hh) * nw * jax.nn.sigmoid(og[b:b + 1, :])).astype(o_ref.dtype)


def ml_step(proj, c0, n0, m0, gate_b, norm_w):
    M = proj.shape[0]
    tb = STEP_TB
    gr, _, pr, _ = _ml_gate_inputs(proj, gate_b)
    kk, kv, ko = ML_QK_DIM // ML_DQK, (2 * ML_QK_DIM) // ML_DV, (2 * ML_QK_DIM + ML_V_DIM) // ML_DV
    c_spec = pl.BlockSpec((tb, 1, ML_DQK, ML_DV), lambda i, h: (i, h, 0, 0))
    n_spec = pl.BlockSpec((tb, 1, 1, ML_DQK), lambda i, h: (i, h, 0, 0))
    m_spec = pl.BlockSpec((tb, 1, 1, LANES), lambda i, h: (i, h, 0, 0))
    n4 = n0.reshape(M, ML_HEADS, 1, ML_DQK)
    m4 = jnp.broadcast_to(m0.reshape(M, ML_HEADS, 1, 1), (M, ML_HEADS, 1, LANES))
    return pl.pallas_call(
        _ml_step_body,
        out_shape=(jax.ShapeDtypeStruct((M, ML_V_DIM), BF16), jax.ShapeDtypeStruct(c0.shape, F32),
                   jax.ShapeDtypeStruct(n4.shape, F32), jax.ShapeDtypeStruct(m4.shape, F32)),
        grid=(M // tb, ML_HEADS),
        in_specs=[
            pl.BlockSpec((tb, ML_DQK), lambda i, h: (i, h)),
            pl.BlockSpec((tb, ML_DQK), lambda i, h: (i, kk + h)),
            pl.BlockSpec((tb, ML_DV), lambda i, h: (i, kv + h)),
            pl.BlockSpec((tb, ML_DV), lambda i, h: (i, ko + h)),
            pl.BlockSpec((1, tb, 2), lambda i, h: (h, i, 0)),
            pl.BlockSpec((1, 1, 2), lambda i, h: (h, 0, 0)),
            pl.BlockSpec((1, ML_DV), lambda i, h: (0, h)),
            c_spec, n_spec, m_spec,
        ],
        out_specs=(pl.BlockSpec((tb, ML_DV), lambda i, h: (i, h)), c_spec, n_spec, m_spec),
        compiler_params=_cparams("parallel", "parallel"),
        name="ml_step",
    )(proj, proj, proj, proj, gr, pr, norm_w.reshape(1, ML_V_DIM), c0, n4, m4)


def _last_rows(proj, nseq, seq_len, n, col0, width):
    n = min(n, seq_len)
    return proj.reshape(nseq, seq_len, proj.shape[1])[:, seq_len - n:, col0:col0 + width]


def _trunk(x, pos0, states, p, w):
    nseq, seq_len, D = x.shape
    M = nseq * seq_len
    seq = states is None
    xs = x.reshape(M, D)
    ffn_bufs = []
    out = {}
    wb = {'ffn_w_gate': [], 'ffn_w_up': [], 'ffn_w_down': []}

    def mm(a, key, res=None, layer=None):
        if seq:
            return matmul(a, w[key] if layer is None else w[key][layer], res=res, name=key)
        o, wb_ = matmul_cast(a, w[key], res=res, layer=layer, name=key)
        if layer is None:
            wb[key] = wb_
        else:
            wb[key].append(wb_)
        return o

    for layer in range(4):
        hn = rmsnorm(xs, p['norm_mix'][layer], BF16)
        if layer == 0:
            proj = mm(hn, 'w_ret_in')
            if seq:
                y, out['ret'] = ret_prompt(proj, nseq, seq_len)
            else:
                y, out['ret'] = ret_step(proj, states['ret'], pos0)
            xs = mm(y, 'w_ret_out', res=xs)
        elif layer == 1:
            proj = mm(hn, 'w_ssm_in')
            raw = _last_rows(proj, nseq, seq_len, SSM_CONV - 1, SSM_D_INNER, SSM_CONV_DIM)
            if seq:
                xa = conv_seq(proj, SSM_D_INNER, p['ssm_conv_w'], p['ssm_conv_b'], nseq, seq_len)
                out['ssm_conv'] = raw
                y, out['ssm'] = ssm_prompt(proj, xa, p['ssm_dt_bias'], p['ssm_a_log'], p['ssm_d_skip'],
                                           p['ssm_norm_w'], nseq, seq_len)
            else:
                xa = conv_step(proj, SSM_D_INNER, p['ssm_conv_w'], p['ssm_conv_b'], states['ssm_conv'])
                out['ssm_conv'] = jnp.concatenate([states['ssm_conv'][:, 1:], raw], axis=1)
                y, out['ssm'] = ssm_step(proj, xa, states['ssm'], p['ssm_dt_bias'], p['ssm_a_log'], p['ssm_d_skip'],
                                         p['ssm_norm_w'])
            xs = mm(y, 'w_ssm_out', res=xs)
        elif layer == 2:
            proj = mm(hn, 'w_gdn_in')
            raw = _last_rows(proj, nseq, seq_len, GDN_CONV - 1, 0, GDN_CONV_DIM)
            zero_b = jnp.zeros((GDN_CONV_DIM,), F32)
            if seq:
                qkv = conv_seq(proj, 0, p['gdn_conv_w'], zero_b, nseq, seq_len)
                out['gdn_conv'] = raw
                y, out['gdn'] = gdn_prompt(proj, qkv, p['gdn_a_log'], p['gdn_dt_bias'], p['gdn_norm_w'], nseq, seq_len)
            else:
                qkv = conv_step(proj, 0, p['gdn_conv_w'], zero_b, states['gdn_conv'])
                out['gdn_conv'] = jnp.concatenate([states['gdn_conv'][:, 1:], raw], axis=1)
                y, out['gdn'] = gdn_step(proj, qkv, states['gdn'], p['gdn_a_log'], p['gdn_dt_bias'], p['gdn_norm_w'])
            xs = mm(y, 'w_gdn_out', res=xs)
        else:
            proj = mm(hn, 'w_ml_in')
            if seq:
                y, c, n, m = ml_prompt(proj, p['ml_gate_b'], p['ml_norm_w'], nseq, seq_len)
            else:
                y, c, n, m = ml_step(proj, states['ml_c'], states['ml_n'], states['ml_m'], p['ml_gate_b'],
                                     p['ml_norm_w'])
            out['ml_c'] = c
            out['ml_n'] = n.reshape(nseq, ML_HEADS, ML_DQK)
            out['ml_m'] = m[:, :, 0, 0]
            xs = mm(y, 'w_ml_out', res=xs)
        hn = rmsnorm(xs, p['norm_ffn'][layer], BF16)
        cw, cbias = p['ffn_conv_w'][layer], p['ffn_conv_b'][layer]
        if seq:
            act, tails = ffn1_seq(hn, w['ffn_w_gate'][layer], w['ffn_w_up'][layer], cw, cbias, seq_len)
            tails = tails.reshape(nseq, -1, SUBLANES, D_FF)
            ffn_bufs.append(tails[:, -1, SUBLANES - (FFN_CONV - 1):])
        else:
            hist = states['ffn'][layer]
            act, g_raw, wgb, wub = ffn1_step(hn, w['ffn_w_gate'], w['ffn_w_up'], cw, cbias, hist[:, 0], hist[:, 1],
                                             layer=layer)
            wb['ffn_w_gate'].append(wgb)
            wb['ffn_w_up'].append(wub)
            ffn_bufs.append(jnp.stack([hist[:, 1], g_raw], axis=1))
        xs = mm(act, 'ffn_w_down', res=xs, layer=layer)
    y = rmsnorm(xs, p['norm_final'], F32).reshape(nseq, seq_len, D)
    return (y, out['ret'], out['ssm'], out['ssm_conv'], out['gdn'], out['gdn_conv'], out['ml_c'], out['ml_n'],
            out['ml_m'], jnp.stack(ffn_bufs)), wb


def kernel(x_prompt, x_sample, state_ret, state_ssm, state_ssm_conv, state_gdn, state_gdn_conv, state_mlstm_c, state_mlstm_n, state_mlstm_m, state_ffn_conv, norm_mix, norm_ffn, norm_final, w_ret_in, w_ret_out, w_ssm_in, ssm_conv_w, ssm_conv_b, ssm_dt_bias, ssm_a_log, ssm_d_skip, ssm_norm_w, w_ssm_out, w_gdn_in, gdn_conv_w, gdn_a_log, gdn_dt_bias, gdn_norm_w, w_gdn_out, w_ml_in, ml_gate_b, ml_norm_w, w_ml_out, ffn_w_gate, ffn_w_up, ffn_conv_w, ffn_conv_b, ffn_w_down):
    p = {
        'norm_mix': norm_mix, 'norm_ffn': norm_ffn, 'norm_final': norm_final,
        'ssm_conv_w': ssm_conv_w, 'ssm_conv_b': ssm_conv_b, 'ssm_dt_bias': ssm_dt_bias, 'ssm_a_log': ssm_a_log,
        'ssm_d_skip': ssm_d_skip, 'ssm_norm_w': ssm_norm_w,
        'gdn_conv_w': gdn_conv_w, 'gdn_a_log': gdn_a_log, 'gdn_dt_bias': gdn_dt_bias, 'gdn_norm_w': gdn_norm_w,
        'ml_gate_b': ml_gate_b, 'ml_norm_w': ml_norm_w,
        'ffn_conv_w': ffn_conv_w, 'ffn_conv_b': ffn_conv_b,
    }
    w32 = {
        'w_ret_in': w_ret_in, 'w_ret_out': w_ret_out, 'w_ssm_in': w_ssm_in, 'w_ssm_out': w_ssm_out,
        'w_gdn_in': w_gdn_in, 'w_gdn_out': w_gdn_out, 'w_ml_in': w_ml_in, 'w_ml_out': w_ml_out,
        'ffn_w_gate': ffn_w_gate, 'ffn_w_up': ffn_w_up, 'ffn_w_down': ffn_w_down,
    }
    states = {
        'ret': state_ret, 'ssm': state_ssm, 'ssm_conv': state_ssm_conv, 'gdn': state_gdn, 'gdn_conv': state_gdn_conv,
        'ml_c': state_mlstm_c, 'ml_n': state_mlstm_n, 'ml_m': state_mlstm_m, 'ffn': state_ffn_conv,
    }
    sample, wb = _trunk(x_sample, PAST_LEN, states, p, w32)
    prompt, _ = _trunk(x_prompt, 0, None, p, wb)
    return (prompt[0], sample[0]) + prompt[1:] + sample[1:]
```

```python
import functools

import jax
import jax.numpy as jnp
from jax import lax
from jax.experimental import pallas as pl
from jax.experimental.pallas import tpu as pltpu

F32 = jnp.float32
BF16 = jnp.bfloat16

D_MODEL = 4096
PAST_LEN = 16384
CHUNK = 128
NORM_EPS = 1e-6
ROPE_BASE = 10000.0

RET_HEADS = 16
RET_DK = 256
RET_DV = 512
RET_QK_DIM = RET_HEADS * RET_DK
RET_V_DIM = RET_HEADS * RET_DV

SSM_D_INNER = 8192
SSM_HEAD_DIM = 64
SSM_HEADS = 128
SSM_STATE = 128
SSM_GROUPS = 8
SSM_HPG = SSM_HEADS // SSM_GROUPS
SSM_GROUP_W = SSM_HPG * SSM_HEAD_DIM
SSM_CONV = 4
SSM_CONV_DIM = SSM_D_INNER + 2 * SSM_GROUPS * SSM_STATE

GDN_K_HEADS = 32
GDN_V_HEADS = 64
GDN_REP = GDN_V_HEADS // GDN_K_HEADS
GDN_DK = 128
GDN_DV = 128
GDN_CONV = 4
GDN_KEY_DIM = GDN_K_HEADS * GDN_DK
GDN_VAL_DIM = GDN_V_HEADS * GDN_DV
GDN_CONV_DIM = 2 * GDN_KEY_DIM + GDN_VAL_DIM

ML_HEADS = 8
ML_DQK = 256
ML_DV = 512
ML_QK_DIM = ML_HEADS * ML_DQK
ML_V_DIM = ML_HEADS * ML_DV

D_FF = 11008
FFN_CONV = 3

VMEM_LIMIT_BYTES = 56 * 1024 * 1024
SUBLANES = 8
LANES = 128


def _cparams(*sem):
    return pltpu.CompilerParams(dimension_semantics=sem, vmem_limit_bytes=VMEM_LIMIT_BYTES)


def _bdot(a, b):
    return jnp.dot(a.astype(BF16), b.astype(BF16), preferred_element_type=F32)


def _bdot_nt(a, b):
    return lax.dot_general(a.astype(BF16), b.astype(BF16), (((1,), (1,)), ((), ())), preferred_element_type=F32)


def _bdot_tn(a, b):
    return lax.dot_general(a.astype(BF16), b.astype(BF16), (((0,), (0,)), ((), ())), preferred_element_type=F32)


def _split2(x):
    hi = x.astype(BF16)
    lo = (x - hi.astype(F32)).astype(BF16)
    return hi, lo


def _split3(x):
    hi = x.astype(BF16)
    r = x - hi.astype(F32)
    mid = r.astype(BF16)
    lo = (r - mid.astype(F32)).astype(BF16)
    return hi, mid, lo


def _dot3(a, b):
    ah, al = _split2(a)
    bh, bl = _split2(b)
    d = functools.partial(jnp.dot, preferred_element_type=F32)
    return d(ah, bh) + (d(ah, bl) + d(al, bh))


def _cumsum_rows(x):
    L = x.shape[0]
    tri = (lax.broadcasted_iota(jnp.int32, (L, L), 0) >= lax.broadcasted_iota(jnp.int32, (L, L), 1)).astype(BF16)
    d = functools.partial(jnp.dot, preferred_element_type=F32)
    hi, mid, lo = _split3(x)
    return d(tri, hi) + (d(tri, mid) + d(tri, lo))


def _cumsum_cols(x):
    L = x.shape[1]
    tri = (lax.broadcasted_iota(jnp.int32, (L, L), 0) <= lax.broadcasted_iota(jnp.int32, (L, L), 1)).astype(BF16)
    d = functools.partial(jnp.dot, preferred_element_type=F32)
    hi, mid, lo = _split3(x)
    return d(hi, tri) + (d(mid, tri) + d(lo, tri))


def _lane_bcast_cols(xt, width=LANES):
    n, tb = xt.shape
    shift = width.bit_length() - 1
    sel = ((lax.broadcasted_iota(jnp.int32, (tb, tb * width), 1) >> shift)
           == lax.broadcasted_iota(jnp.int32, (tb, tb * width), 0)).astype(BF16)
    d = functools.partial(jnp.dot, preferred_element_type=F32)
    hi, mid, lo = _split3(xt)
    return d(hi, sel) + (d(mid, sel) + d(lo, sel))


def _softplus(x):
    return jnp.maximum(x, 0.0) + jnp.log1p(jnp.exp(-jnp.abs(x)))


def _silu(x):
    return x * jax.nn.sigmoid(x)


def _rms(x):
    return x * lax.rsqrt(jnp.mean(x * x, axis=-1, keepdims=True) + NORM_EPS)


def _tri_masks(L):
    ri = lax.broadcasted_iota(jnp.int32, (L, L), 0)
    ci = lax.broadcasted_iota(jnp.int32, (L, L), 1)
    return ri, ci


def _rmsnorm_body(x_ref, w_ref, o_ref):
    o_ref[...] = (_rms(x_ref[...]) * w_ref[...]).astype(o_ref.dtype)


def rmsnorm(x, w, out_dtype):
    M, D = x.shape
    tm = min(M, 256)
    return pl.pallas_call(
        _rmsnorm_body,
        out_shape=jax.ShapeDtypeStruct((M, D), out_dtype),
        grid=(M // tm,),
        in_specs=[pl.BlockSpec((tm, D), lambda i: (i, 0)), pl.BlockSpec((1, D), lambda i: (0, 0))],
        out_specs=pl.BlockSpec((tm, D), lambda i: (i, 0)),
        compiler_params=_cparams("parallel"),
        name="rmsnorm",
    )(x, w.reshape(1, D))


def _mm_body(a_ref, b_ref, o_ref):
    o_ref[...] = jnp.dot(a_ref[...], b_ref[...], preferred_element_type=F32)


def _mm_res_body(a_ref, b_ref, r_ref, o_ref):
    o_ref[...] = r_ref[...] + jnp.dot(a_ref[...], b_ref[...], preferred_element_type=F32)


def _mm_tiles(M, K):
    if M <= 128:
        return M, (1024 if K <= 4096 else 512), False
    tm = min(M, 1024)
    if K <= 4096:
        return tm, 512, False
    if K <= 8192:
        return tm, 256, False
    return tm, 256, True


def _w_spec(w, layer, rows, tn, index):
    if w.ndim == 2:
        return pl.BlockSpec((rows, tn), lambda *g: (0, index(*g)))
    return pl.BlockSpec((None, rows, tn), lambda *g: (layer, 0, index(*g)))


def matmul(a, b, res=None, layer=None, name="matmul"):
    M, K = a.shape
    N = b.shape[-1]
    tm, tn, single = _mm_tiles(M, K)
    tn = min(tn, N)
    a_spec = pl.BlockSpec((tm, K), lambda i, j: (i, 0), **({"pipeline_mode": pl.Buffered(1)} if single else {}))
    in_specs = [a_spec, _w_spec(b, layer, K, tn, lambda i, j: j)]
    args = [a, b]
    body = _mm_body
    if res is not None:
        in_specs.append(pl.BlockSpec((tm, tn), lambda i, j: (i, j)))
        args.append(res)
        body = _mm_res_body
    return pl.pallas_call(
        body,
        out_shape=jax.ShapeDtypeStruct((M, N), F32),
        grid=(M // tm, pl.cdiv(N, tn)),
        in_specs=in_specs,
        out_specs=pl.BlockSpec((tm, tn), lambda i, j: (i, j)),
        compiler_params=_cparams("parallel", "parallel"),
        name=name,
    )(*args)


def _mm_cast_body(a_ref, w_ref, o_ref, wb_ref):
    wt = w_ref[...].astype(BF16)
    wb_ref[...] = wt
    o_ref[...] = jnp.dot(a_ref[...], wt, preferred_element_type=F32)


def _mm_cast_res_body(a_ref, w_ref, r_ref, o_ref, wb_ref):
    wt = w_ref[...].astype(BF16)
    wb_ref[...] = wt
    o_ref[...] = r_ref[...] + jnp.dot(a_ref[...], wt, preferred_element_type=F32)


def matmul_cast(a, w, res=None, layer=None, name="matmul_cast"):
    M, K = a.shape
    N = w.shape[-1]
    tn = 512 if K <= 4096 else 256
    in_specs = [pl.BlockSpec((M, K), lambda j: (0, 0)), _w_spec(w, layer, K, tn, lambda j: j)]
    args = [a, w]
    body = _mm_cast_body
    if res is not None:
        in_specs.append(pl.BlockSpec((M, tn), lambda j: (0, j)))
        args.append(res)
        body = _mm_cast_res_body
    return pl.pallas_call(
        body,
        out_shape=(jax.ShapeDtypeStruct((M, N), F32), jax.ShapeDtypeStruct((K, N), BF16)),
        grid=(pl.cdiv(N, tn),),
        in_specs=in_specs,
        out_specs=(pl.BlockSpec((M, tn), lambda j: (0, j)), pl.BlockSpec((K, tn), lambda j: (0, j))),
        compiler_params=_cparams("parallel"),
        name=name,
    )(*args)


FFN_TC = 256
FFN_TN = 2 * FFN_TC


def _ffn1_seq_body(x_ref, wg_ref, wu_ref, cw_ref, cb_ref, act_ref, tail_ref, carry_ref, gbuf_ref, *, tiles_per_seq):
    i = pl.program_id(0)
    j = pl.program_id(1)
    tm, tn = act_ref.shape
    tc = FFN_TC

    @pl.when((i == 0) & (j == 0))
    def _():
        carry_ref[...] = jnp.zeros(carry_ref.shape, F32)

    x = x_ref[...]
    for c in range(tn // tc):
        cs = slice(c * tc, (c + 1) * tc)
        g = jnp.dot(x, wg_ref[:, cs], preferred_element_type=F32)
        u = jnp.dot(x, wu_ref[:, cs], preferred_element_type=F32)
        gbuf_ref[c, 0:SUBLANES, :] = jnp.where(i % tiles_per_seq == 0, 0.0, carry_ref[j, :, cs])
        gbuf_ref[c, SUBLANES:SUBLANES + tm, :] = g
        w = cw_ref[:, cs]
        y = (gbuf_ref[c, SUBLANES - 2:SUBLANES - 2 + tm, :] * w[0:1]
             + gbuf_ref[c, SUBLANES - 1:SUBLANES - 1 + tm, :] * w[1:2] + g * w[2:3] + cb_ref[:, cs])
        act_ref[:, cs] = (_silu(y) * u).astype(act_ref.dtype)
        tail = g[tm - SUBLANES:tm, :]
        carry_ref[j, :, cs] = tail
        tail_ref[0, :, cs] = tail


def ffn1_seq(x, wg, wu, cw, cb, seq_len, layer=None):
    M, D = x.shape
    F = wg.shape[-1]
    tm = min(seq_len, 1024)
    tn = FFN_TN
    nj = pl.cdiv(F, tn)
    return pl.pallas_call(
        functools.partial(_ffn1_seq_body, tiles_per_seq=seq_len // tm),
        out_shape=(jax.ShapeDtypeStruct((M, F), BF16), jax.ShapeDtypeStruct((M // tm, SUBLANES, F), F32)),
        grid=(M // tm, nj),
        in_specs=[
            pl.BlockSpec((tm, D), lambda i, j: (i, 0)),
            _w_spec(wg, layer, D, tn, lambda i, j: j),
            _w_spec(wu, layer, D, tn, lambda i, j: j),
            pl.BlockSpec((FFN_CONV, tn), lambda i, j: (0, j)),
            pl.BlockSpec((1, tn), lambda i, j: (0, j)),
        ],
        out_specs=(pl.BlockSpec((tm, tn), lambda i, j: (i, j)), pl.BlockSpec((1, SUBLANES, tn), lambda i, j: (i, 0, j))),
        scratch_shapes=[pltpu.VMEM((nj, SUBLANES, tn), F32), pltpu.VMEM((tn // FFN_TC, tm + SUBLANES, FFN_TC), F32)],
        compiler_params=_cparams("arbitrary", "arbitrary"),
        name="ffn1_seq",
    )(x, wg, wu, cw, cb.reshape(1, F))


def _ffn1_step_body(x_ref, wg_ref, wu_ref, cw_ref, cb_ref, s0_ref, s1_ref, act_ref, g_ref, wgb_ref, wub_ref):
    x = x_ref[...]
    wg = wg_ref[...].astype(BF16)
    wu = wu_ref[...].astype(BF16)
    wgb_ref[...] = wg
    wub_ref[...] = wu
    g = jnp.dot(x, wg, preferred_element_type=F32)
    u = jnp.dot(x, wu, preferred_element_type=F32)
    w = cw_ref[...]
    y = s0_ref[...] * w[0:1] + s1_ref[...] * w[1:2] + g * w[2:3] + cb_ref[...]
    act_ref[...] = (_silu(y) * u).astype(act_ref.dtype)
    g_ref[...] = g


def ffn1_step(x, wg, wu, cw, cb, s0, s1, layer=None):
    M, D = x.shape
    F = wg.shape[-1]
    tn = FFN_TC
    col = lambda j: (0, j)
    return pl.pallas_call(
        _ffn1_step_body,
        out_shape=(jax.ShapeDtypeStruct((M, F), BF16), jax.ShapeDtypeStruct((M, F), F32),
                   jax.ShapeDtypeStruct((D, F), BF16), jax.ShapeDtypeStruct((D, F), BF16)),
        grid=(pl.cdiv(F, tn),),
        in_specs=[
            pl.BlockSpec((M, D), lambda j: (0, 0)),
            _w_spec(wg, layer, D, tn, lambda j: j),
            _w_spec(wu, layer, D, tn, lambda j: j),
            pl.BlockSpec((FFN_CONV, tn), col),
            pl.BlockSpec((1, tn), col),
            pl.BlockSpec((M, tn), col),
            pl.BlockSpec((M, tn), col),
        ],
        out_specs=(pl.BlockSpec((M, tn), col), pl.BlockSpec((M, tn), col),
                   pl.BlockSpec((D, tn), col), pl.BlockSpec((D, tn), col)),
        compiler_params=_cparams("parallel"),
        name="ffn1_step",
    )(x, wg, wu, cw, cb.reshape(1, F), s0, s1)


CONV_TC = 1024


def _conv_seq_body(x_ref, w_ref, b_ref, o_ref, buf_ref, *, width):
    t = pl.program_id(2)
    tt = x_ref.shape[0]

    @pl.when(t == 0)
    def _():
        buf_ref[0:SUBLANES, :] = jnp.zeros((SUBLANES, buf_ref.shape[1]), F32)

    x = x_ref[...]
    buf_ref[SUBLANES:SUBLANES + tt, :] = x
    w = w_ref[...]
    y = x * w[width - 1:width]
    for j in range(1, width):
        y = y + buf_ref[SUBLANES - j:SUBLANES - j + tt, :] * w[width - 1 - j:width - j]
    o_ref[...] = _silu(y + b_ref[...])
    buf_ref[0:SUBLANES, :] = x[tt - SUBLANES:tt, :]


def conv_seq(proj, col0, w, b, nseq, seq_len):
    width, C = w.shape
    tc = CONV_TC
    tt = min(seq_len, 512)
    nt = seq_len // tt
    c0 = col0 // tc
    return pl.pallas_call(
        functools.partial(_conv_seq_body, width=width),
        out_shape=jax.ShapeDtypeStruct((nseq * seq_len, C), F32),
        grid=(nseq, C // tc, nt),
        in_specs=[
            pl.BlockSpec((tt, tc), lambda s, c, t: (s * nt + t, c0 + c)),
            pl.BlockSpec((width, tc), lambda s, c, t: (0, c)),
            pl.BlockSpec((1, tc), lambda s, c, t: (0, c)),
        ],
        out_specs=pl.BlockSpec((tt, tc), lambda s, c, t: (s * nt + t, c)),
        scratch_shapes=[pltpu.VMEM((tt + SUBLANES, tc), F32)],
        compiler_params=_cparams("parallel", "parallel", "arbitrary"),
        name="conv_seq",
    )(proj, w, b.reshape(1, C))


def _conv_step_body(x_ref, s0_ref, s1_ref, s2_ref, w_ref, b_ref, o_ref):
    w = w_ref[...]
    y = s0_ref[...] * w[0:1] + s1_ref[...] * w[1:2] + s2_ref[...] * w[2:3] + x_ref[...] * w[3:4] + b_ref[...]
    o_ref[...] = _silu(y)


def conv_step(proj, col0, w, b, hist):
    width, C = w.shape
    M = proj.shape[0]
    tc = CONV_TC
    c0 = col0 // tc
    spec = pl.BlockSpec((M, tc), lambda c: (0, c))
    return pl.pallas_call(
        _conv_step_body,
        out_shape=jax.ShapeDtypeStruct((M, C), F32),
        grid=(C // tc,),
        in_specs=[pl.BlockSpec((M, tc), lambda c: (0, c0 + c)), spec, spec, spec,
                  pl.BlockSpec((width, tc), lambda c: (0, c)), pl.BlockSpec((1, tc), lambda c: (0, c))],
        out_specs=spec,
        compiler_params=_cparams("parallel"),
        name="conv_step",
    )(proj, hist[:, 0], hist[:, 1], hist[:, 2], w, b.reshape(1, C))


def _ret_log_gamma(h):
    return jnp.log(1.0 - jnp.exp2(-5.0 - jnp.full((1, 1), h, jnp.int32).astype(F32)))


def _rope(x, cos, sin):
    half = x.shape[-1] // 2
    x1, x2 = x[:, :half], x[:, half:]
    return jnp.concatenate([x1 * cos - x2 * sin, x1 * sin + x2 * cos], axis=-1)


RET_H_STEP = 8


def _ret_chunk_body(q_ref, k_ref, v_ref, g_ref, cos_ref, sin_ref, o_ref, s_ref):
    hg = pl.program_id(1)
    c = pl.program_id(2)
    L = q_ref.shape[0]
    heads = range(RET_H_STEP)

    @pl.when(c == 0)
    def _():
        s_ref[...] = jnp.zeros(s_ref.shape, F32)

    ri, ci = _tri_masks(L)
    rel = (ri - ci).astype(F32)
    idx = lax.broadcasted_iota(jnp.int32, (L, 1), 0).astype(F32)
    cos, sin = cos_ref[...], sin_ref[...]
    lg = [_ret_log_gamma(hg * RET_H_STEP + e) for e in heads]
    q = [_rope(q_ref[:, e * RET_DK:(e + 1) * RET_DK], cos, sin) for e in heads]
    k = [_rope(k_ref[:, e * RET_DK:(e + 1) * RET_DK], cos, sin) * (RET_DK ** -0.5) for e in heads]
    v = [v_ref[:, e * RET_DV:(e + 1) * RET_DV] for e in heads]
    s = [s_ref[0, e] for e in heads]
    sc = [_bdot_nt(q[e], k[e]) * jnp.where(rel >= 0, jnp.exp(lg[e] * jnp.maximum(rel, 0.0)), 0.0) for e in heads]
    o = [_bdot(sc[e], v[e]) + _bdot(q[e], s[e]) * jnp.exp((idx + 1.0) * lg[e]) for e in heads]
    s_new = [jnp.exp(L * lg[e]) * s[e] + _bdot_tn(k[e] * jnp.exp((L - 1.0 - idx) * lg[e]), v[e]) for e in heads]
    for e in heads:
        s_ref[0, e] = s_new[e]
        o_ref[:, e * RET_DV:(e + 1) * RET_DV] = (
            _silu(g_ref[:, e * RET_DV:(e + 1) * RET_DV]) * _rms(o[e])).astype(o_ref.dtype)


def _rope_tables(pos):
    half = RET_DK // 2
    inv_freq = 1.0 / (ROPE_BASE ** jnp.linspace(0.0, 1.0, half, dtype=F32))
    ang = pos.astype(F32)[:, None] * inv_freq[None, :]
    return jnp.cos(ang), jnp.sin(ang)


def ret_prompt(proj, nseq, seq_len):
    L = min(CHUNK, seq_len)
    nc = seq_len // L
    M = nseq * seq_len
    cos, sin = _rope_tables(jnp.arange(seq_len))
    hs = RET_H_STEP
    KW, VW = hs * RET_DK, hs * RET_DV
    kq, kv, kg = RET_QK_DIM // KW, (2 * RET_QK_DIM) // VW, (2 * RET_QK_DIM + RET_V_DIM) // VW
    return pl.pallas_call(
        _ret_chunk_body,
        out_shape=(jax.ShapeDtypeStruct((M, RET_V_DIM), BF16),
                   jax.ShapeDtypeStruct((nseq, RET_HEADS, RET_DK, RET_DV), F32)),
        grid=(nseq, RET_HEADS // hs, nc),
        in_specs=[
            pl.BlockSpec((L, KW), lambda b, h, c: (b * nc + c, h)),
            pl.BlockSpec((L, KW), lambda b, h, c: (b * nc + c, kq + h)),
            pl.BlockSpec((L, VW), lambda b, h, c: (b * nc + c, kv + h)),
            pl.BlockSpec((L, VW), lambda b, h, c: (b * nc + c, kg + h)),
            pl.BlockSpec((L, RET_DK // 2), lambda b, h, c: (c, 0)),
            pl.BlockSpec((L, RET_DK // 2), lambda b, h, c: (c, 0)),
        ],
        out_specs=(pl.BlockSpec((L, VW), lambda b, h, c: (b * nc + c, h)),
                   pl.BlockSpec((1, hs, RET_DK, RET_DV), lambda b, h, c: (b, h, 0, 0))),
        compiler_params=_cparams("parallel", "parallel", "arbitrary"),
        name="ret_chunk",
    )(proj, proj, proj, proj, cos, sin)


STEP_TB = 8


def _ret_step_body(q_ref, k_ref, v_ref, g_ref, cos_ref, sin_ref, s_ref, o_ref, so_ref):
    h = pl.program_id(1)
    tb = q_ref.shape[0]
    gamma = jnp.exp(_ret_log_gamma(h))
    cos, sin = cos_ref[...], sin_ref[...]
    q = _rope(q_ref[...], cos, sin)
    k = _rope(k_ref[...], cos, sin) * (RET_DK ** -0.5)
    v = v_ref[...]
    g = g_ref[...]
    sc = jnp.sum(q * k, axis=-1, keepdims=True)
    qt = q.T
    kt = k.T
    for b in range(tb):
        s = s_ref[b, 0]
        vb = v[b:b + 1, :]
        qs = jnp.sum(s * qt[:, b:b + 1], axis=0, keepdims=True)
        o = sc[b:b + 1, :] * vb + qs * gamma
        so_ref[b, 0] = gamma * s + kt[:, b:b + 1] * vb
        o_ref[b:b + 1, :] = (_silu(g[b:b + 1, :]) * _rms(o)).astype(o_ref.dtype)


def ret_step(proj, state, pos):
    M = proj.shape[0]
    tb = STEP_TB
    cos, sin = _rope_tables(jnp.full((1,), pos))
    kq, kv, kg = RET_QK_DIM // RET_DK, (2 * RET_QK_DIM) // RET_DV, (2 * RET_QK_DIM + RET_V_DIM) // RET_DV
    s_spec = pl.BlockSpec((tb, 1, RET_DK, RET_DV), lambda i, h: (i, h, 0, 0))
    return pl.pallas_call(
        _ret_step_body,
        out_shape=(jax.ShapeDtypeStruct((M, RET_V_DIM), BF16), jax.ShapeDtypeStruct(state.shape, F32)),
        grid=(M // tb, RET_HEADS),
        in_specs=[
            pl.BlockSpec((tb, RET_DK), lambda i, h: (i, h)),
            pl.BlockSpec((tb, RET_DK), lambda i, h: (i, kq + h)),
            pl.BlockSpec((tb, RET_DV), lambda i, h: (i, kv + h)),
            pl.BlockSpec((tb, RET_DV), lambda i, h: (i, kg + h)),
            pl.BlockSpec((1, RET_DK // 2), lambda i, h: (0, 0)),
            pl.BlockSpec((1, RET_DK // 2), lambda i, h: (0, 0)),
            s_spec,
        ],
        out_specs=(pl.BlockSpec((tb, RET_DV), lambda i, h: (i, h)), s_spec),
        compiler_params=_cparams("parallel", "parallel"),
        name="ret_step",
    )(proj, proj, proj, proj, cos, sin, state)


def _ssm_chunk_body(x_ref, b_ref, c_ref, z_ref, dtr_ref, dtc_ref, pr_ref, pc_ref, dskip_ref, nw_ref, y_ref, s_ref):
    c = pl.program_id(2)
    L = x_ref.shape[0]
    P2 = 2 * SSM_HEAD_DIM

    @pl.when(c == 0)
    def _():
        s_ref[...] = jnp.zeros(s_ref.shape, F32)

    pr = pr_ref[0]
    pc = pc_ref[0]
    dt_r = _softplus(dtr_ref[0] + pr[0:1, :])
    dt_c = _softplus(dtc_ref[0] + pc[:, 0:1])
    acs_r = _cumsum_rows(dt_r * (-jnp.exp(pr[1:2, :])))
    acs_c = _cumsum_cols(dt_c * (-jnp.exp(pc[:, 1:2])))
    bm = b_ref[...]
    cm = c_ref[...]
    x = x_ref[...]
    cb = _bdot_nt(cm, bm)
    ri, ci = _tri_masks(L)
    incl = ri >= ci
    lo_lane = lax.broadcasted_iota(jnp.int32, (L, P2), 1) < SSM_HEAD_DIM
    lo_row = lax.broadcasted_iota(jnp.int32, (P2, P2), 0) < SSM_HEAD_DIM
    a_end = acs_r[L - 1:L, :]
    acs_b = _lane_bcast_cols(acs_r, LANES)
    ea2 = _lane_bcast_cols(jnp.exp(acs_r), SSM_HEAD_DIM)
    we2 = _lane_bcast_cols(jnp.exp(a_end - acs_r) * dt_r, SSM_HEAD_DIM)
    e_end = jnp.broadcast_to(jnp.exp(acs_c[:, L - 1:L]), (SSM_HPG, P2))
    pairs = range(SSM_HPG // 2)
    sc = [[cb * jnp.exp(jnp.where(incl, acs_b[:, h * LANES:(h + 1) * LANES] - acs_c[h:h + 1, :], -jnp.inf))
           * dt_c[h:h + 1, :] for h in (2 * p, 2 * p + 1)] for p in pairs]
    xp = [x[:, p * P2:(p + 1) * P2] for p in pairs]
    x2 = [jnp.concatenate([jnp.where(lo_lane, xp[p], 0.0), jnp.where(lo_lane, 0.0, xp[p])], axis=0) for p in pairs]
    st = [s_ref[0, 2 * p:2 * p + 2].reshape(P2, SSM_STATE) for p in pairs]
    ys = [_bdot(jnp.concatenate(sc[p], axis=1), x2[p]) + _bdot_nt(cm, st[p]) * ea2[:, p * P2:(p + 1) * P2]
          for p in pairs]
    st_new = [jnp.where(lo_row, e_end[2 * p:2 * p + 1, :], e_end[2 * p + 1:2 * p + 2, :]) * st[p]
              + _bdot_tn(xp[p] * we2[:, p * P2:(p + 1) * P2], bm) for p in pairs]
    for p in pairs:
        s_ref[0, 2 * p:2 * p + 2] = st_new[p].reshape(2, SSM_HEAD_DIM, SSM_STATE)
    y = jnp.concatenate(ys, axis=1) + dskip_ref[...] * x
    y = y * _silu(z_ref[...])
    y_ref[...] = (_rms(y) * nw_ref[...]).astype(y_ref.dtype)


def _ssm_params(dt_bias, a_log):
    pr = jnp.stack([dt_bias.reshape(SSM_GROUPS, SSM_HPG), a_log.reshape(SSM_GROUPS, SSM_HPG)], axis=1)
    return pr, jnp.swapaxes(pr, 1, 2)


def ssm_prompt(proj, xa, dt_bias, a_log, d_skip, norm_w, nseq, seq_len):
    L = min(CHUNK, seq_len)
    nc = seq_len // L
    M = nseq * seq_len
    W = SSM_GROUP_W
    dt_raw = proj[:, SSM_D_INNER + SSM_CONV_DIM:]
    dtr = jnp.transpose(dt_raw.reshape(M, SSM_GROUPS, SSM_HPG), (1, 0, 2))
    dtc = jnp.swapaxes(dtr, 1, 2)
    pr, pc = _ssm_params(dt_bias, a_log)
    dskip = jnp.repeat(d_skip, SSM_HEAD_DIM).reshape(1, SSM_D_INNER)
    kb = SSM_D_INNER // SSM_STATE
    return pl.pallas_call(
        _ssm_chunk_body,
        out_shape=(jax.ShapeDtypeStruct((M, SSM_D_INNER), BF16),
                   jax.ShapeDtypeStruct((nseq, SSM_HEADS, SSM_HEAD_DIM, SSM_STATE), F32)),
        grid=(nseq, SSM_GROUPS, nc),
        in_specs=[
            pl.BlockSpec((L, W), lambda b, g, c: (b * nc + c, g)),
            pl.BlockSpec((L, SSM_STATE), lambda b, g, c: (b * nc + c, kb + g)),
            pl.BlockSpec((L, SSM_STATE), lambda b, g, c: (b * nc + c, kb + SSM_GROUPS + g)),
            pl.BlockSpec((L, W), lambda b, g, c: (b * nc + c, g)),
            pl.BlockSpec((1, L, SSM_HPG), lambda b, g, c: (g, b * nc + c, 0)),
            pl.BlockSpec((1, SSM_HPG, L), lambda b, g, c: (g, 0, b * nc + c)),
            pl.BlockSpec((1, 2, SSM_HPG), lambda b, g, c: (g, 0, 0)),
            pl.BlockSpec((1, SSM_HPG, 2), lambda b, g, c: (g, 0, 0)),
            pl.BlockSpec((1, W), lambda b, g, c: (0, g)),
            pl.BlockSpec((1, W), lambda b, g, c: (0, g)),
        ],
        out_specs=(pl.BlockSpec((L, W), lambda b, g, c: (b * nc + c, g)),
                   pl.BlockSpec((1, SSM_HPG, SSM_HEAD_DIM, SSM_STATE), lambda b, g, c: (b, g, 0, 0))),
        compiler_params=_cparams("parallel", "parallel", "arbitrary"),
        name="ssm_chunk",
    )(xa, xa, xa, proj, dtr, dtc, pr, pc, dskip, norm_w.reshape(1, SSM_D_INNER))


def _ssm_step_body(x_ref, b_ref, c_ref, z_ref, dtx_ref, bias_ref, alog_ref, dskip_ref, nw_ref, s_ref, y_ref, so_ref):
    tb = x_ref.shape[0]
    W = SSM_GROUP_W
    x = x_ref[...]
    bm = b_ref[...]
    cm = c_ref[...]
    dt = _softplus(dtx_ref[...] + bias_ref[...])
    da = jnp.exp(dt * (-jnp.exp(alog_ref[...])))
    cb = jnp.sum(cm * bm, axis=-1, keepdims=True)
    xdt = (x * dt).T
    dat = da.T
    rows = lax.broadcasted_iota(jnp.int32, (tb, W), 0)
    cs = jnp.zeros((tb, W), F32)
    for b in range(tb):
        st = s_ref[b].reshape(W, SSM_STATE)
        cs = jnp.where(rows == b, _bdot_nt(cm, st), cs)
        st_new = dat[:, b:b + 1] * st + xdt[:, b:b + 1] * bm[b:b + 1, :]
        so_ref[b] = st_new.reshape(SSM_HPG, SSM_HEAD_DIM, SSM_STATE)
    y = cb * dt * x + cs * da + dskip_ref[...] * x
    y = y * _silu(z_ref[...])
    y_ref[...] = (_rms(y) * nw_ref[...]).astype(y_ref.dtype)


def ssm_step(proj, xa, state, dt_bias, a_log, d_skip, norm_w):
    M = proj.shape[0]
    tb = STEP_TB
    W = SSM_GROUP_W
    rep = lambda a: jnp.repeat(a, SSM_HEAD_DIM, axis=-1)
    dtx = rep(proj[:, SSM_D_INNER + SSM_CONV_DIM:])
    kb = SSM_D_INNER // SSM_STATE
    s_spec = pl.BlockSpec((tb, SSM_HPG, SSM_HEAD_DIM, SSM_STATE), lambda g, i: (i, g, 0, 0))
    row = pl.BlockSpec((1, W), lambda g, i: (0, g))
    tile = pl.BlockSpec((tb, W), lambda g, i: (i, g))
    return pl.pallas_call(
        _ssm_step_body,
        out_shape=(jax.ShapeDtypeStruct((M, SSM_D_INNER), BF16), jax.ShapeDtypeStruct(state.shape, F32)),
        grid=(SSM_GROUPS, M // tb),
        in_specs=[
            tile,
            pl.BlockSpec((tb, SSM_STATE), lambda g, i: (i, kb + g)),
            pl.BlockSpec((tb, SSM_STATE), lambda g, i: (i, kb + SSM_GROUPS + g)),
            tile, tile, row, row, row, row, s_spec,
        ],
        out_specs=(tile, s_spec),
        compiler_params=_cparams("parallel", "parallel"),
        name="ssm_step",
    )(xa, xa, xa, proj, dtx, rep(dt_bias).reshape(1, -1), rep(a_log).reshape(1, -1), rep(d_skip).reshape(1, -1),
      norm_w.reshape(1, -1), state)


def _l2n(x):
    return x * lax.rsqrt(jnp.sum(x * x, axis=-1, keepdims=True) + NORM_EPS)


def _unit_lower_inverses(mats, ri, ci):
    L = mats[0].shape[0]
    eye = (ri == ci).astype(F32)
    same8 = (ri >> 3) == (ci >> 3)
    ns = [jnp.where(same8, -a, 0.0) for a in mats]
    n2 = [_bdot(n, n) for n in ns]
    n4 = [_bdot(x, x) for x in n2]
    ts = [eye + n for n in ns]
    ts = [t + _bdot(t, x) for t, x in zip(ts, n2)]
    ts = [t + _bdot(t, x) for t, x in zip(ts, n4)]
    sh = 3
    while (1 << sh) < L:
        lower_left = ((ri >> (sh + 1)) == (ci >> (sh + 1))) & ((ri >> sh) != (ci >> sh))
        us = [_bdot(jnp.where(lower_left, a, 0.0), t) for a, t in zip(mats, ts)]
        ts = [t - _bdot(t, u) for t, u in zip(ts, us)]
        sh += 1
    return ts


def _gdn_chunk_body(q_ref, k_ref, v_ref, z_ref, gr_ref, gc_ref, pr_ref, pc_ref, nw_ref, o_ref, s_ref, *, ks):
    c = pl.program_id(2)
    L = q_ref.shape[0]
    vs = ks * GDN_REP

    @pl.when(c == 0)
    def _():
        s_ref[...] = jnp.zeros(s_ref.shape, F32)

    gr = gr_ref[0]
    gc = gc_ref[0]
    pr = pr_ref[0]
    pc = pc_ref[0]
    beta = jax.nn.sigmoid(gr[:, 0:vs])
    gt_r = _cumsum_rows(-jnp.exp(pr[0:1, :]) * _softplus(gr[:, vs:] + pr[1:2, :]))
    gt_c = _cumsum_cols(-jnp.exp(pc[:, 0:1]) * _softplus(gc[vs:, :] + pc[:, 1:2]))
    ri, ci = _tri_masks(L)
    nw = nw_ref[...]
    qs, ks_ = [], []
    for kh in range(ks):
        qs.append(_l2n(q_ref[:, kh * GDN_DK:(kh + 1) * GDN_DK]) * (GDN_DK ** -0.5))
        ks_.append(_l2n(k_ref[:, kh * GDN_DK:(kh + 1) * GDN_DK]))
    kks = [_bdot_nt(k, k) for k in ks_]
    qks = [_bdot_nt(q, k) for q, k in zip(qs, ks_)]
    heads = range(vs)
    kof = [j // GDN_REP for j in heads]
    st = [s_ref[0, j] for j in heads]
    gcol = [gt_r[:, j:j + 1] for j in heads]
    seg = [gcol[j] - gt_c[j:j + 1, :] for j in heads]
    bcol = [beta[:, j:j + 1] for j in heads]
    eg = [jnp.exp(gcol[j]) for j in heads]
    a_mats = [kks[kof[j]] * jnp.exp(jnp.where(ri > ci, seg[j], -jnp.inf)) * bcol[j] for j in heads]
    rhs = [bcol[j] * (v_ref[:, j * GDN_DV:(j + 1) * GDN_DV] - eg[j] * _bdot(ks_[kof[j]], st[j])) for j in heads]
    tinv = _unit_lower_inverses(a_mats, ri, ci)
    delta = [_bdot(tinv[j], rhs[j]) for j in heads]
    o = [eg[j] * _bdot(qs[kof[j]], st[j])
         + _bdot(qks[kof[j]] * jnp.exp(jnp.where(ri >= ci, seg[j], -jnp.inf)), delta[j]) for j in heads]
    g_end = [gt_r[L - 1:L, j:j + 1] for j in heads]
    s_new = [jnp.exp(g_end[j]) * st[j] + _bdot_tn(ks_[kof[j]] * jnp.exp(g_end[j] - gcol[j]), delta[j]) for j in heads]
    for j in heads:
        s_ref[0, j] = s_new[j]
        ze = z_ref[:, j * GDN_DV:(j + 1) * GDN_DV]
        o_ref[:, j * GDN_DV:(j + 1) * GDN_DV] = (_rms(o[j]) * nw * _silu(ze)).astype(o_ref.dtype)


def _gdn_gate_inputs(proj, ks):
    M = proj.shape[0]
    ng, vs = GDN_K_HEADS // ks, ks * GDN_REP
    tail = proj[:, GDN_CONV_DIM + GDN_VAL_DIM:]
    braw = tail[:, :GDN_V_HEADS].reshape(M, ng, vs)
    araw = tail[:, GDN_V_HEADS:].reshape(M, ng, vs)
    gr = jnp.transpose(jnp.concatenate([braw, araw], axis=-1), (1, 0, 2))
    return gr, jnp.swapaxes(gr, 1, 2)


def _gdn_params(a_log, dt_bias, ks):
    ng, vs = GDN_K_HEADS // ks, ks * GDN_REP
    pr = jnp.stack([a_log.reshape(ng, vs), dt_bias.reshape(ng, vs)], axis=1)
    return pr, jnp.swapaxes(pr, 1, 2)


GDN_KH_STEP = 16


def gdn_prompt(proj, qkv, a_log, dt_bias, norm_w, nseq, seq_len):
    L = min(CHUNK, seq_len)
    nc = seq_len // L
    M = nseq * seq_len
    ks = GDN_KH_STEP
    ng, vs = GDN_K_HEADS // ks, ks * GDN_REP
    KW, VW = ks * GDN_DK, vs * GDN_DV
    gr, gc = _gdn_gate_inputs(proj, ks)
    pr, pc = _gdn_params(a_log, dt_bias, ks)
    kk, kv, kz = GDN_KEY_DIM // KW, (2 * GDN_KEY_DIM) // VW, GDN_CONV_DIM // VW
    return pl.pallas_call(
        functools.partial(_gdn_chunk_body, ks=ks),
        out_shape=(jax.ShapeDtypeStruct((M, GDN_VAL_DIM), BF16),
                   jax.ShapeDtypeStruct((nseq, GDN_V_HEADS, GDN_DK, GDN_DV), F32)),
        grid=(nseq, ng, nc),
        in_specs=[
            pl.BlockSpec((L, KW), lambda b, h, c: (b * nc + c, h)),
            pl.BlockSpec((L, KW), lambda b, h, c: (b * nc + c, kk + h)),
            pl.BlockSpec((L, VW), lambda b, h, c: (b * nc + c, kv + h)),
            pl.BlockSpec((L, VW), lambda b, h, c: (b * nc + c, kz + h)),
            pl.BlockSpec((1, L, 2 * vs), lambda b, h, c: (h, b * nc + c, 0)),
            pl.BlockSpec((1, 2 * vs, L), lambda b, h, c: (h, 0, b * nc + c)),
            pl.BlockSpec((1, 2, vs), lambda b, h, c: (h, 0, 0)),
            pl.BlockSpec((1, vs, 2), lambda b, h, c: (h, 0, 0)),
            pl.BlockSpec((1, GDN_DV), lambda b, h, c: (0, 0)),
        ],
        out_specs=(pl.BlockSpec((L, VW), lambda b, h, c: (b * nc + c, h)),
                   pl.BlockSpec((1, vs, GDN_DK, GDN_DV), lambda b, h, c: (b, h, 0, 0))),
        compiler_params=_cparams("parallel", "parallel", "arbitrary"),
        name="gdn_chunk",
    )(qkv, qkv, qkv, proj, gr, gc, pr, pc, norm_w.reshape(1, GDN_DV))


def _gdn_step_body(q_ref, k_ref, v_ref, z_ref, gr_ref, pr_ref, nw_ref, s_ref, o_ref, so_ref, *, ks):
    tb = q_ref.shape[0]
    vs = ks * GDN_REP
    gr = gr_ref[0]
    pr = pr_ref[0]
    beta = jax.nn.sigmoid(gr[:, 0:vs])
    eg_all = jnp.exp(-jnp.exp(pr[0:1, :]) * _softplus(gr[:, vs:] + pr[1:2, :]))
    nw = nw_ref[...]
    for kh in range(ks):
        q = _l2n(q_ref[:, kh * GDN_DK:(kh + 1) * GDN_DK]) * (GDN_DK ** -0.5)
        k = _l2n(k_ref[:, kh * GDN_DK:(kh + 1) * GDN_DK])
        qk = jnp.sum(q * k, axis=-1, keepdims=True)
        qb = _lane_bcast_cols(q.T)
        kb = _lane_bcast_cols(k.T)
        for b in range(tb):
            kcol = kb[:, b * LANES:(b + 1) * LANES]
            qcol = qb[:, b * LANES:(b + 1) * LANES]
            for e in range(GDN_REP):
                j = kh * GDN_REP + e
                s = s_ref[b, j]
                eg = eg_all[b:b + 1, j:j + 1]
                k_s = jnp.sum(s * kcol, axis=0, keepdims=True)
                q_s = jnp.sum(s * qcol, axis=0, keepdims=True)
                delta = beta[b:b + 1, j:j + 1] * (v_ref[b:b + 1, j * GDN_DV:(j + 1) * GDN_DV] - eg * k_s)
                o = eg * q_s + qk[b:b + 1, :] * delta
                so_ref[b, j] = eg * s + kcol * delta
                ze = z_ref[b:b + 1, j * GDN_DV:(j + 1) * GDN_DV]
                o_ref[b:b + 1, j * GDN_DV:(j + 1) * GDN_DV] = (_rms(o) * nw * _silu(ze)).astype(o_ref.dtype)


GDN_KH_STEP_DECODE = 4


def gdn_step(proj, qkv, state, a_log, dt_bias, norm_w):
    M = proj.shape[0]
    tb = STEP_TB
    ks = GDN_KH_STEP_DECODE
    ng, vs = GDN_K_HEADS // ks, ks * GDN_REP
    KW, VW = ks * GDN_DK, vs * GDN_DV
    gr, _ = _gdn_gate_inputs(proj, ks)
    pr, _ = _gdn_params(a_log, dt_bias, ks)
    kk, kv, kz = GDN_KEY_DIM // KW, (2 * GDN_KEY_DIM) // VW, GDN_CONV_DIM // VW
    s_spec = pl.BlockSpec((tb, vs, GDN_DK, GDN_DV), lambda i, h: (i, h, 0, 0))
    return pl.pallas_call(
        functools.partial(_gdn_step_body, ks=ks),
        out_shape=(jax.ShapeDtypeStruct((M, GDN_VAL_DIM), BF16), jax.ShapeDtypeStruct(state.shape, F32)),
        grid=(M // tb, ng),
        in_specs=[
            pl.BlockSpec((tb, KW), lambda i, h: (i, h)),
            pl.BlockSpec((tb, KW), lambda i, h: (i, kk + h)),
            pl.BlockSpec((tb, VW), lambda i, h: (i, kv + h)),
            pl.BlockSpec((tb, VW), lambda i, h: (i, kz + h)),
            pl.BlockSpec((1, tb, 2 * vs), lambda i, h: (h, i, 0)),
            pl.BlockSpec((1, 2, vs), lambda i, h: (h, 0, 0)),
            pl.BlockSpec((1, GDN_DV), lambda i, h: (0, 0)),
            s_spec,
        ],
        out_specs=(pl.BlockSpec((tb, VW), lambda i, h: (i, h)), s_spec),
        compiler_params=_cparams("parallel", "parallel"),
        name="gdn_step",
    )(qkv, qkv, qkv, proj, gr, pr, norm_w.reshape(1, GDN_DV), state)


def _log_sigmoid(x):
    return -_softplus(-x)


ML_H_STEP = 8


def _ml_chunk_body(q_ref, k_ref, v_ref, og_ref, gr_ref, gc_ref, pr_ref, pc_ref, nw_ref, o_ref, c_ref, n_ref, m_ref):
    ch = pl.program_id(2)
    L = q_ref.shape[0]

    @pl.when(ch == 0)
    def _():
        c_ref[...] = jnp.zeros(c_ref.shape, F32)
        n_ref[...] = jnp.zeros(n_ref.shape, F32)
        m_ref[...] = jnp.zeros(m_ref.shape, F32)

    ri, ci = _tri_masks(L)
    H = range(ML_H_STEP)
    q = [q_ref[:, e * ML_DQK:(e + 1) * ML_DQK] for e in H]
    k = [k_ref[:, e * ML_DQK:(e + 1) * ML_DQK] * (ML_DQK ** -0.5) for e in H]
    v = [v_ref[:, e * ML_DV:(e + 1) * ML_DV] for e in H]
    gr = [gr_ref[e] + pr_ref[e] for e in H]
    gc = [gc_ref[e] + pc_ref[e] for e in H]
    it_c = [gr[e][:, 0:1] for e in H]
    it_r = [gc[e][0:1, :] for e in H]
    ft_c = [_cumsum_rows(_log_sigmoid(gr[e][:, 1:2])) for e in H]
    ft_r = [_cumsum_cols(_log_sigmoid(gc[e][1:2, :])) for e in H]
    c_st = [c_ref[0, e] for e in H]
    n_st = [n_ref[0, e] for e in H]
    m_st = [m_ref[0, e][:, 0:1] for e in H]
    dmat = [jnp.where(ri >= ci, ft_c[e] - ft_r[e] + it_r[e], -jnp.inf) for e in H]
    inter = [ft_c[e] + m_st[e] for e in H]
    mi = [jnp.maximum(inter[e], jnp.max(dmat[e], axis=-1, keepdims=True)) for e in H]
    winter = [jnp.exp(inter[e] - mi[e]) for e in H]
    sc = [_bdot_nt(q[e], k[e]) * jnp.exp(dmat[e] - mi[e]) for e in H]
    num = [winter[e] * _bdot(q[e], c_st[e]) + _bdot(sc[e], v[e]) for e in H]
    den = [winter[e] * jnp.sum(q[e] * n_st[e], axis=-1, keepdims=True) + jnp.sum(sc[e], axis=-1, keepdims=True)
           for e in H]
    hh = [num[e] / jnp.maximum(jnp.abs(den[e]), jnp.exp(-mi[e])) for e in H]
    m_new = [mi[e][L - 1:L, :] for e in H]
    f_end = [ft_c[e][L - 1:L, :] for e in H]
    keep = [jnp.exp(f_end[e] + m_st[e] - m_new[e]) for e in H]
    kw = [k[e] * jnp.exp(f_end[e] - ft_c[e] + it_c[e] - m_new[e]) for e in H]
    c_new = [keep[e] * c_st[e] + _bdot_tn(kw[e], v[e]) for e in H]
    for e in H:
        c_ref[0, e] = c_new[e]
        n_ref[0, e] = keep[e] * n_st[e] + jnp.sum(kw[e], axis=0, keepdims=True)
        m_ref[0, e] = jnp.broadcast_to(m_new[e], (1, LANES))
        vs = slice(e * ML_DV, (e + 1) * ML_DV)
        o_ref[:, vs] = (_rms(hh[e]) * nw_ref[:, vs] * jax.nn.sigmoid(og_ref[:, vs])).astype(o_ref.dtype)


def _ml_gate_inputs(proj, gate_b):
    M = proj.shape[0]
    tail = proj[:, 2 * ML_QK_DIM + 2 * ML_V_DIM:]
    gr = jnp.transpose(tail.reshape(M, 2, ML_HEADS), (2, 0, 1))
    pr = jnp.transpose(gate_b.reshape(2, ML_HEADS), (1, 0)).reshape(ML_HEADS, 1, 2)
    return gr, jnp.swapaxes(gr, 1, 2), pr, jnp.swapaxes(pr, 1, 2)


def ml_prompt(proj, gate_b, norm_w, nseq, seq_len):
    L = min(CHUNK, seq_len)
    nc = seq_len // L
    M = nseq * seq_len
    gr, gc, pr, pc = _ml_gate_inputs(proj, gate_b)
    hs = ML_H_STEP
    KW, VW = hs * ML_DQK, hs * ML_DV
    kk, kv, ko = ML_QK_DIM // KW, (2 * ML_QK_DIM) // VW, (2 * ML_QK_DIM + ML_V_DIM) // VW
    return pl.pallas_call(
        _ml_chunk_body,
        out_shape=(jax.ShapeDtypeStruct((M, ML_V_DIM), BF16),
                   jax.ShapeDtypeStruct((nseq, ML_HEADS, ML_DQK, ML_DV), F32),
                   jax.ShapeDtypeStruct((nseq, ML_HEADS, 1, ML_DQK), F32),
                   jax.ShapeDtypeStruct((nseq, ML_HEADS, 1, LANES), F32)),
        grid=(nseq, ML_HEADS // hs, nc),
        in_specs=[
            pl.BlockSpec((L, KW), lambda b, h, c: (b * nc + c, h)),
            pl.BlockSpec((L, KW), lambda b, h, c: (b * nc + c, kk + h)),
            pl.BlockSpec((L, VW), lambda b, h, c: (b * nc + c, kv + h)),
            pl.BlockSpec((L, VW), lambda b, h, c: (b * nc + c, ko + h)),
            pl.BlockSpec((hs, L, 2), lambda b, h, c: (h, b * nc + c, 0)),
            pl.BlockSpec((hs, 2, L), lambda b, h, c: (h, 0, b * nc + c)),
            pl.BlockSpec((hs, 1, 2), lambda b, h, c: (h, 0, 0)),
            pl.BlockSpec((hs, 2, 1), lambda b, h, c: (h, 0, 0)),
            pl.BlockSpec((1, VW), lambda b, h, c: (0, h)),
        ],
        out_specs=(pl.BlockSpec((L, VW), lambda b, h, c: (b * nc + c, h)),
                   pl.BlockSpec((1, hs, ML_DQK, ML_DV), lambda b, h, c: (b, h, 0, 0)),
                   pl.BlockSpec((1, hs, 1, ML_DQK), lambda b, h, c: (b, h, 0, 0)),
                   pl.BlockSpec((1, hs, 1, LANES), lambda b, h, c: (b, h, 0, 0))),
        compiler_params=_cparams("parallel", "parallel", "arbitrary"),
        name="ml_chunk",
    )(proj, proj, proj, proj, gr, gc, pr, pc, norm_w.reshape(1, ML_V_DIM))


def _ml_step_body(q_ref, k_ref, v_ref, og_ref, gr_ref, pr_ref, nw_ref, c_ref, n_ref, m_ref, o_ref, co_ref, no_ref, mo_ref):
    tb = q_ref.shape[0]
    q = q_ref[...]
    k = k_ref[...] * (ML_DQK ** -0.5)
    v = v_ref[...]
    og = og_ref[...]
    gr = gr_ref[0] + pr_ref[0]
    it = gr[:, 0:1]
    ft = _log_sigmoid(gr[:, 1:2])
    qk = jnp.sum(q * k, axis=-1, keepdims=True)
    qt = _lane_bcast_cols(q.T)
    kt = _lane_bcast_cols(k.T)
    nw = nw_ref[...]
    reps = ML_DV // LANES
    for b in range(tb):
        qcol = jnp.concatenate([qt[:, b * LANES:(b + 1) * LANES]] * reps, axis=1)
        kcol = jnp.concatenate([kt[:, b * LANES:(b + 1) * LANES]] * reps, axis=1)
        c_st = c_ref[b, 0]
        n_st = n_ref[b, 0]
        m_st = m_ref[b, 0][:, 0:1]
        qb = q[b:b + 1, :]
        kb = k[b:b + 1, :]
        vb = v[b:b + 1, :]
        itb = it[b:b + 1, :]
        ftb = ft[b:b + 1, :]
        inter = ftb + m_st
        mi = jnp.maximum(inter, itb)
        wmat = jnp.exp(itb - mi)
        winter = jnp.exp(inter - mi)
        sc = qk[b:b + 1, :] * wmat
        num = winter * jnp.sum(c_st * qcol, axis=0, keepdims=True) + sc * vb
        den = winter * jnp.sum(qb * n_st, axis=-1, keepdims=True) + sc
        hh = num / jnp.maximum(jnp.abs(den), jnp.exp(-mi))
        wk = jnp.exp(itb - mi)
        keep = jnp.exp(ftb + m_st - mi)
        co_ref[b, 0] = keep * c_st + (kcol * wk) * vb
        no_ref[b, 0] = keep * n_st + kb * wk
        mo_ref[b, 0] = jnp.broadcast_to(mi, (1, LANES))
        o_ref[b:b + 1, :] = (_rms(hh) * nw * jax.nn.sigmoid(og[b:b + 1, :])).astype(o_ref.dtype)


def ml_step(proj, c0, n0, m0, gate_b, norm_w):
    M = proj.shape[0]
    tb = STEP_TB
    gr, _, pr, _ = _ml_gate_inputs(proj, gate_b)
    kk, kv, ko = ML_QK_DIM // ML_DQK, (2 * ML_QK_DIM) // ML_DV, (2 * ML_QK_DIM + ML_V_DIM) // ML_DV
    c_spec = pl.BlockSpec((tb, 1, ML_DQK, ML_DV), lambda i, h: (i, h, 0, 0))
    n_spec = pl.BlockSpec((tb, 1, 1, ML_DQK), lambda i, h: (i, h, 0, 0))
    m_spec = pl.BlockSpec((tb, 1, 1, LANES), lambda i, h: (i, h, 0, 0))
    n4 = n0.reshape(M, ML_HEADS, 1, ML_DQK)
    m4 = jnp.broadcast_to(m0.reshape(M, ML_HEADS, 1, 1), (M, ML_HEADS, 1, LANES))
    return pl.pallas_call(
        _ml_step_body,
        out_shape=(jax.ShapeDtypeStruct((M, ML_V_DIM), BF16), jax.ShapeDtypeStruct(c0.shape, F32),
                   jax.ShapeDtypeStruct(n4.shape, F32), jax.ShapeDtypeStruct(m4.shape, F32)),
        grid=(M // tb, ML_HEADS),
        in_specs=[
            pl.BlockSpec((tb, ML_DQK), lambda i, h: (i, h)),
            pl.BlockSpec((tb, ML_DQK), lambda i, h: (i, kk + h)),
            pl.BlockSpec((tb, ML_DV), lambda i, h: (i, kv + h)),
            pl.BlockSpec((tb, ML_DV), lambda i, h: (i, ko + h)),
            pl.BlockSpec((1, tb, 2), lambda i, h: (h, i, 0)),
            pl.BlockSpec((1, 1, 2), lambda i, h: (h, 0, 0)),
            pl.BlockSpec((1, ML_DV), lambda i, h: (0, h)),
            c_spec, n_spec, m_spec,
        ],
        out_specs=(pl.BlockSpec((tb, ML_DV), lambda i, h: (i, h)), c_spec, n_spec, m_spec),
        compiler_params=_cparams("parallel", "parallel"),
        name="ml_step",
    )(proj, proj, proj, proj, gr, pr, norm_w.reshape(1, ML_V_DIM), c0, n4, m4)


def _last_rows(proj, nseq, seq_len, n, col0, width):
    n = min(n, seq_len)
    return proj.reshape(nseq, seq_len, proj.shape[1])[:, seq_len - n:, col0:col0 + width]


def _trunk(x, pos0, states, p, w):
    nseq, seq_len, D = x.shape
    M = nseq * seq_len
    seq = states is None
    xs = x.reshape(M, D)
    ffn_bufs = []
    out = {}
    wb = {'ffn_w_gate': [], 'ffn_w_up': [], 'ffn_w_down': []}

    def mm(a, key, res=None, layer=None):
        if seq:
            return matmul(a, w[key] if layer is None else w[key][layer], res=res, name=key)
        o, wb_ = matmul_cast(a, w[key], res=res, layer=layer, name=key)
        if layer is None:
            wb[key] = wb_
        else:
            wb[key].append(wb_)
        return o

    for layer in range(4):
        hn = rmsnorm(xs, p['norm_mix'][layer], BF16)
        if layer == 0:
            proj = mm(hn, 'w_ret_in')
            if seq:
                y, out['ret'] = ret_prompt(proj, nseq, seq_len)
            else:
                y, out['ret'] = ret_step(proj, states['ret'], pos0)
            xs = mm(y, 'w_ret_out', res=xs)
        elif layer == 1:
            proj = mm(hn, 'w_ssm_in')
            raw = _last_rows(proj, nseq, seq_len, SSM_CONV - 1, SSM_D_INNER, SSM_CONV_DIM)
            if seq:
                xa = conv_seq(proj, SSM_D_INNER, p['ssm_conv_w'], p['ssm_conv_b'], nseq, seq_len)
                out['ssm_conv'] = raw
                y, out['ssm'] = ssm_prompt(proj, xa, p['ssm_dt_bias'], p['ssm_a_log'], p['ssm_d_skip'],
                                           p['ssm_norm_w'], nseq, seq_len)
            else:
                xa = conv_step(proj, SSM_D_INNER, p['ssm_conv_w'], p['ssm_conv_b'], states['ssm_conv'])
                out['ssm_conv'] = jnp.concatenate([states['ssm_conv'][:, 1:], raw], axis=1)
                y, out['ssm'] = ssm_step(proj, xa, states['ssm'], p['ssm_dt_bias'], p['ssm_a_log'], p['ssm_d_skip'],
                                         p['ssm_norm_w'])
            xs = mm(y, 'w_ssm_out', res=xs)
        elif layer == 2:
            proj = mm(hn, 'w_gdn_in')
            raw = _last_rows(proj, nseq, seq_len, GDN_CONV - 1, 0, GDN_CONV_DIM)
            zero_b = jnp.zeros((GDN_CONV_DIM,), F32)
            if seq:
                qkv = conv_seq(proj, 0, p['gdn_conv_w'], zero_b, nseq, seq_len)
                out['gdn_conv'] = raw
                y, out['gdn'] = gdn_prompt(proj, qkv, p['gdn_a_log'], p['gdn_dt_bias'], p['gdn_norm_w'], nseq, seq_len)
            else:
                qkv = conv_step(proj, 0, p['gdn_conv_w'], zero_b, states['gdn_conv'])
                out['gdn_conv'] = jnp.concatenate([states['gdn_conv'][:, 1:], raw], axis=1)
                y, out['gdn'] = gdn_step(proj, qkv, states['gdn'], p['gdn_a_log'], p['gdn_dt_bias'], p['gdn_norm_w'])
            xs = mm(y, 'w_gdn_out', res=xs)
        else:
            proj = mm(hn, 'w_ml_in')
            if seq:
                y, c, n, m = ml_prompt(proj, p['ml_gate_b'], p['ml_norm_w'], nseq, seq_len)
            else:
                y, c, n, m = ml_step(proj, states['ml_c'], states['ml_n'], states['ml_m'], p['ml_gate_b'],
                                     p['ml_norm_w'])
            out['ml_c'] = c
            out['ml_n'] = n.reshape(nseq, ML_HEADS, ML_DQK)
            out['ml_m'] = m[:, :, 0, 0]
            xs = mm(y, 'w_ml_out', res=xs)
        hn = rmsnorm(xs, p['norm_ffn'][layer], BF16)
        cw, cbias = p['ffn_conv_w'][layer], p['ffn_conv_b'][layer]
        if seq:
            act, tails = ffn1_seq(hn, w['ffn_w_gate'][layer], w['ffn_w_up'][layer], cw, cbias, seq_len)
            tails = tails.reshape(nseq, -1, SUBLANES, D_FF)
            ffn_bufs.append(tails[:, -1, SUBLANES - (FFN_CONV - 1):])
        else:
            hist = states['ffn'][layer]
            act, g_raw, wgb, wub = ffn1_step(hn, w['ffn_w_gate'], w['ffn_w_up'], cw, cbias, hist[:, 0], hist[:, 1],
                                             layer=layer)
            wb['ffn_w_gate'].append(wgb)
            wb['ffn_w_up'].append(wub)
            ffn_bufs.append(jnp.stack([hist[:, 1], g_raw], axis=1))
        xs = mm(act, 'ffn_w_down', res=xs, layer=layer)
    y = rmsnorm(xs, p['norm_final'], F32).reshape(nseq, seq_len, D)
    return (y, out['ret'], out['ssm'], out['ssm_conv'], out['gdn'], out['gdn_conv'], out['ml_c'], out['ml_n'],
            out['ml_m'], jnp.stack(ffn_bufs)), wb


def kernel(x_prompt, x_sample, state_ret, state_ssm, state_ssm_conv, state_gdn, state_gdn_conv, state_mlstm_c, state_mlstm_n, state_mlstm_m, state_ffn_conv, norm_mix, norm_ffn, norm_final, w_ret_in, w_ret_out, w_ssm_in, ssm_conv_w, ssm_conv_b, ssm_dt_bias, ssm_a_log, ssm_d_skip, ssm_norm_w, w_ssm_out, w_gdn_in, gdn_conv_w, gdn_a_log, gdn_dt_bias, gdn_norm_w, w_gdn_out, w_ml_in, ml_gate_b, ml_norm_w, w_ml_out, ffn_w_gate, ffn_w_up, ffn_conv_w, ffn_conv_b, ffn_w_down):
    p = {
        'norm_mix': norm_mix, 'norm_ffn': norm_ffn, 'norm_final': norm_final,
        'ssm_conv_w': ssm_conv_w, 'ssm_conv_b': ssm_conv_b, 'ssm_dt_bias': ssm_dt_bias, 'ssm_a_log': ssm_a_log,
        'ssm_d_skip': ssm_d_skip, 'ssm_norm_w': ssm_norm_w,
        'gdn_conv_w': gdn_conv_w, 'gdn_a_log': gdn_a_log, 'gdn_dt_bias': gdn_dt_bias, 'gdn_norm_w': gdn_norm_w,
        'ml_gate_b': ml_gate_b, 'ml_norm_w': ml_norm_w,
        'ffn_conv_w': ffn_conv_w, 'ffn_conv_b': ffn_conv_b,
    }
    w32 = {
        'w_ret_in': w_ret_in, 'w_ret_out': w_ret_out, 'w_ssm_in': w_ssm_in, 'w_ssm_out': w_ssm_out,
        'w_gdn_in': w_gdn_in, 'w_gdn_out': w_gdn_out, 'w_ml_in': w_ml_in, 'w_ml_out': w_ml_out,
        'ffn_w_gate': ffn_w_gate, 'ffn_w_up': ffn_w_up, 'ffn_w_down': ffn_w_down,
    }
    states = {
        'ret': state_ret, 'ssm': state_ssm, 'ssm_conv': state_ssm_conv, 'gdn': state_gdn, 'gdn_conv': state_gdn_conv,
        'ml_c': state_mlstm_c, 'ml_n': state_mlstm_n, 'ml_m': state_mlstm_m, 'ffn': state_ffn_conv,
    }
    sample, wb = _trunk(x_sample, PAST_LEN, states, p, w32)
    prompt, _ = _trunk(x_prompt, 0, None, p, wb)
    return (prompt[0], sample[0]) + prompt[1:] + sample[1:]
```
